```python
import jax, jax.numpy as jnp
from jax import lax
import numpy as np

D_MODEL = 1024
BATCH = 4
SEQ = 4096
DEPTH = 1
DEC_BATCH = 128
DEC_SEQ = 4
PAST_LEN = 2048
PAGE_SIZE = 128

N_META = 16
GLA_HEADS = 4
GLA_DK = D_MODEL // 8
GLA_DV = D_MODEL // 4
GLA_RANK = 16
GLA_TAU = 16.0
GLA_CHUNK = 64
SB_HEAD_DIM = 64
SB_HEADS = D_MODEL // SB_HEAD_DIM
SB_QBLOCK = 128
SB_BIAS_INIT = -6.0
D_FF = ((8 * D_MODEL // 3 + 255) // 256) * 256
EPS = 1e-6

QK_GLA = GLA_HEADS * GLA_DK
V_GLA = GLA_HEADS * GLA_DV
SB_W = SB_HEADS * SB_HEAD_DIM
SPLITS = (QK_GLA, QK_GLA, V_GLA, GLA_RANK, V_GLA, SB_W, SB_W, SB_W, D_MODEL, D_MODEL)
D_IN = sum(SPLITS)

kernel_name = "hybrid_gla_stickbreak_macaron_step"


def rms_norm(x, g):
    xf = x.astype(jnp.float32)
    y = xf * lax.rsqrt(jnp.mean(xf * xf, axis=-1, keepdims=True) + EPS)
    return (y * g.astype(jnp.float32)).astype(x.dtype)


def swiglu(x, w_gate, w_up, w_down):
    return (jax.nn.silu(x @ w_gate) * (x @ w_up)) @ w_down


def gla_chunked(q, k, v, g, s0, chunk):
    B, L, H, dk = q.shape
    dv = v.shape[-1]
    n = L // chunk
    q, k, v, g = [t.reshape(B, n, chunk, H, t.shape[-1]) for t in (q, k, v, g)]
    b = jnp.cumsum(g, axis=2)
    b_last = b[:, :, -1:]
    q_in = q * jnp.exp(b)
    k_in = k * jnp.exp(-b)
    k_end = k * jnp.exp(b_last - b)
    causal = jnp.tril(jnp.ones((chunk, chunk), dtype=bool))
    a = jnp.where(causal, jnp.einsum('bnthd,bnshd->bnhts', q_in, k_in), 0.0)
    o = jnp.einsum('bnhts,bnshe->bnthe', a, v)
    ds = jnp.einsum('bnshd,bnshe->nbhde', k_end, v)
    decay = jnp.exp(b_last[:, :, 0]).transpose(1, 0, 2, 3)

    def step(s, inp):
        dec, d = inp
        return dec[..., None] * s + d, s

    s_fin, s_in = lax.scan(step, s0, (decay, ds))
    o = o + jnp.einsum('bnthd,nbhde->bnthe', q_in, s_in)
    return o.reshape(B, L, H, dv), s_fin


def gla_branch(q, k, v, g, s0, lead):
    L = q.shape[1]
    outs = []
    s = s0
    if lead > 0:
        o1, s = gla_chunked(q[:, :lead], k[:, :lead], v[:, :lead], g[:, :lead], s, lead)
        outs.append(o1)
    rest = L - lead
    chunk = GLA_CHUNK if rest % GLA_CHUNK == 0 else rest
    o2, s = gla_chunked(q[:, lead:], k[:, lead:], v[:, lead:], g[:, lead:], s, chunk)
    outs.append(o2)
    o = outs[0] if len(outs) == 1 else jnp.concatenate(outs, axis=1)
    return o, s


def sb_attend(q, k, v, bias, q_off):
    B, Lq, H, d = q.shape
    Lk = k.shape[1]
    qb = min(SB_QBLOCK, Lq)
    n_blk = -(-Lq // qb)
    pad = n_blk * qb - Lq
    qp = jnp.pad(q, ((0, 0), (0, pad), (0, 0), (0, 0)))
    qp = qp.reshape(B, n_blk, qb, H, d).transpose(1, 0, 2, 3, 4)
    q_idx = (q_off + jnp.arange(n_blk * qb, dtype=jnp.int32)).reshape(n_blk, qb)
    k_idx = jnp.arange(Lk, dtype=jnp.int32)
    scale = d ** -0.5
    bias_f = bias.astype(jnp.float32)[None, :, None, None]

    def block(args):
        qblk, qi = args
        z = jnp.einsum('bqhd,bkhd->bhqk', qblk, k).astype(jnp.float32) * scale + bias_f
        mask = k_idx[None, :] < qi[:, None]
        l = jnp.where(mask, jax.nn.log_sigmoid(-z), 0.0)
        r = lax.cumsum(l, axis=3, reverse=True) - l
        a = jnp.where(mask, jnp.exp(jax.nn.log_sigmoid(z) + r), 0.0)
        return jnp.einsum('bhqk,bkhd->bqhd', a.astype(v.dtype), v)

    o = lax.map(block, (qp, q_idx))
    return o.transpose(1, 0, 2, 3, 4).reshape(B, n_blk * qb, H, d)[:, :Lq]


def layer(x, s0, past_k, past_v, lead,
          ffn1_pre_g, ffn1_w_gate, ffn1_w_up, ffn1_w_down, ffn1_post_g,
          mix_pre_g, w_in, w_gk2, b_gk, gla_norm_g, sb_bias, w_o_gla, w_o_sb, w_out, mix_post_g,
          ffn2_pre_g, ffn2_w_gate, ffn2_w_up, ffn2_w_down, ffn2_post_g):
    B, L, _ = x.shape
    f32 = jnp.float32
    h = x + 0.5 * rms_norm(swiglu(rms_norm(x, ffn1_pre_g), ffn1_w_gate, ffn1_w_up, ffn1_w_down), ffn1_post_g)
    u = rms_norm(h, mix_pre_g)
    proj = u @ w_in
    offs = []
    acc = 0
    for sz in SPLITS[:-1]:
        acc += sz
        offs.append(acc)
    q_a, k_a, v_a, gk_low, r_a, q_b, k_b, v_b, gate_a, gate_b = jnp.split(proj, offs, axis=-1)
    q_a = q_a.reshape(B, L, GLA_HEADS, GLA_DK).astype(f32) * (GLA_DK ** -0.5)
    k_a = k_a.reshape(B, L, GLA_HEADS, GLA_DK).astype(f32)
    v_a = v_a.reshape(B, L, GLA_HEADS, GLA_DV).astype(f32)
    g = jax.nn.log_sigmoid((gk_low @ w_gk2 + b_gk).astype(f32)) / GLA_TAU
    g = g.reshape(B, L, GLA_HEADS, GLA_DK)
    o_a, s_new = gla_branch(q_a, k_a, v_a, g, s0.astype(f32), lead)
    o_a = rms_norm(o_a, gla_norm_g).reshape(B, L, V_GLA).astype(x.dtype) * jax.nn.silu(r_a)
    q_b = q_b.reshape(B, L, SB_HEADS, SB_HEAD_DIM)
    k_b = k_b.reshape(B, L, SB_HEADS, SB_HEAD_DIM)
    v_b = v_b.reshape(B, L, SB_HEADS, SB_HEAD_DIM)
    if past_k is None:
        keys, vals, q_off = k_b, v_b, 0
    else:
        keys = jnp.concatenate([past_k.astype(k_b.dtype), k_b], axis=1)
        vals = jnp.concatenate([past_v.astype(v_b.dtype), v_b], axis=1)
        q_off = past_k.shape[1]
    o_b = sb_attend(q_b, keys, vals, sb_bias, q_off).reshape(B, L, SB_W)
    m = jax.nn.sigmoid(gate_a) * (o_a @ w_o_gla) + jax.nn.sigmoid(gate_b) * (o_b @ w_o_sb)
    h = h + rms_norm(m @ w_out, mix_post_g)
    y = h + 0.5 * rms_norm(swiglu(rms_norm(h, ffn2_pre_g), ffn2_w_gate, ffn2_w_up, ffn2_w_down), ffn2_post_g)
    return y, k_b, v_b, s_new


def setup_inputs(seed: int = 0) -> dict:
    key = jax.random.key(seed)
    ks = jax.random.split(key, 32)
    n_pages = PAST_LEN // PAGE_SIZE
    n_pool = (DEC_BATCH * n_pages * 5) // 4
    f32 = jnp.float32

    def nrm(k, shape, scale):
        return jax.random.normal(k, shape, f32) * scale

    def gain(k, n):
        return 1.0 + 0.02 * jax.random.normal(k, (DEPTH, n), f32)

    page_table = jax.random.permutation(ks[5], n_pool)[:DEC_BATCH * n_pages]
    page_table = page_table.reshape(DEC_BATCH, n_pages).astype(jnp.int32)
    return {
        "x_prompt": nrm(ks[0], (BATCH, SEQ, D_MODEL), 1.0),
        "x_sample": nrm(ks[1], (DEC_BATCH, DEC_SEQ, D_MODEL), 1.0),
        "cache_k": nrm(ks[2], (DEPTH, n_pool, PAGE_SIZE, SB_HEADS, SB_HEAD_DIM), 1.0),
        "cache_v": nrm(ks[3], (DEPTH, n_pool, PAGE_SIZE, SB_HEADS, SB_HEAD_DIM), 1.0),
        "state_gla": nrm(ks[4], (DEPTH, DEC_BATCH, GLA_HEADS, GLA_DK, GLA_DV), 1.0),
        "page_table": page_table,
        "meta_tokens": nrm(ks[6], (N_META, D_MODEL), 1.0),
        "ffn1_pre_g": gain(ks[7], D_MODEL),
        "ffn1_w_gate": nrm(ks[8], (DEPTH, D_MODEL, D_FF), D_MODEL ** -0.5),
        "ffn1_w_up": nrm(ks[9], (DEPTH, D_MODEL, D_FF), D_MODEL ** -0.5),
        "ffn1_w_down": nrm(ks[10], (DEPTH, D_FF, D_MODEL), D_FF ** -0.5),
        "ffn1_post_g": gain(ks[11], D_MODEL),
        "mix_pre_g": gain(ks[12], D_MODEL),
        "w_in": nrm(ks[13], (DEPTH, D_MODEL, D_IN), D_MODEL ** -0.5),
        "w_gk2": nrm(ks[14], (DEPTH, GLA_RANK, QK_GLA), GLA_RANK ** -0.5),
        "b_gk": nrm(ks[15], (DEPTH, QK_GLA), 0.01),
        "gla_norm_g": gain(ks[16], GLA_DV),
        "sb_bias": SB_BIAS_INIT + 0.1 * jax.random.normal(ks[26], (DEPTH, SB_HEADS), f32),
        "w_o_gla": nrm(ks[17], (DEPTH, V_GLA, D_MODEL), V_GLA ** -0.5),
        "w_o_sb": nrm(ks[18], (DEPTH, SB_W, D_MODEL), SB_W ** -0.5),
        "w_out": nrm(ks[19], (DEPTH, D_MODEL, D_MODEL), D_MODEL ** -0.5),
        "mix_post_g": gain(ks[20], D_MODEL),
        "ffn2_pre_g": gain(ks[21], D_MODEL),
        "ffn2_w_gate": nrm(ks[22], (DEPTH, D_MODEL, D_FF), D_MODEL ** -0.5),
        "ffn2_w_up": nrm(ks[23], (DEPTH, D_MODEL, D_FF), D_MODEL ** -0.5),
        "ffn2_w_down": nrm(ks[24], (DEPTH, D_FF, D_MODEL), D_FF ** -0.5),
        "ffn2_post_g": gain(ks[25], D_MODEL),
    }


def reference(x_prompt, x_sample, cache_k, cache_v, state_gla, page_table, meta_tokens,
              ffn1_pre_g, ffn1_w_gate, ffn1_w_up, ffn1_w_down, ffn1_post_g,
              mix_pre_g, w_in, w_gk2, b_gk, gla_norm_g, sb_bias, w_o_gla, w_o_sb, w_out, mix_post_g,
              ffn2_pre_g, ffn2_w_gate, ffn2_w_up, ffn2_w_down, ffn2_post_g):
    B = x_prompt.shape[0]
    DB = x_sample.shape[0]
    meta = jnp.broadcast_to(meta_tokens.astype(x_prompt.dtype)[None], (B, N_META, x_prompt.shape[-1]))
    xp = jnp.concatenate([meta, x_prompt], axis=1)
    xs = x_sample
    kp_l, vp_l, sp_l, ks_l, vs_l, ss_l = [], [], [], [], [], []
    for l in range(DEPTH):
        w = (ffn1_pre_g[l], ffn1_w_gate[l], ffn1_w_up[l], ffn1_w_down[l], ffn1_post_g[l],
             mix_pre_g[l], w_in[l], w_gk2[l], b_gk[l], gla_norm_g[l], sb_bias[l], w_o_gla[l], w_o_sb[l],
             w_out[l], mix_post_g[l], ffn2_pre_g[l], ffn2_w_gate[l], ffn2_w_up[l], ffn2_w_down[l],
             ffn2_post_g[l])
        s0 = jnp.zeros((B, GLA_HEADS, GLA_DK, GLA_DV), jnp.float32)
        xp, kp, vp, sp = layer(xp, s0, None, None, N_META, *w)
        past_k = cache_k[l][page_table].reshape(DB, -1, SB_HEADS, SB_HEAD_DIM)
        past_v = cache_v[l][page_table].reshape(DB, -1, SB_HEADS, SB_HEAD_DIM)
        xs, ks, vs, ss = layer(xs, state_gla[l], past_k, past_v, 0, *w)
        kp_l.append(kp); vp_l.append(vp); sp_l.append(sp)
        ks_l.append(ks); vs_l.append(vs); ss_l.append(ss)
    y_prompt = xp[:, N_META:]
    y_sample = xs
    k_prompt = jnp.stack(kp_l)
    v_prompt = jnp.stack(vp_l)
    gla_state_prompt = jnp.stack(sp_l)
    k_sample = jnp.stack(ks_l)
    v_sample = jnp.stack(vs_l)
    gla_state_sample = jnp.stack(ss_l)
    return (y_prompt, y_sample, k_prompt, v_prompt, gla_state_prompt, k_sample, v_sample, gla_state_sample)
```

```python
import functools

import jax
import jax.numpy as jnp
from jax import lax
from jax.experimental import pallas as pl
from jax.experimental.pallas import tpu as pltpu

F32 = jnp.float32
BF16 = jnp.bfloat16

EPS = 1e-6
N_META = 16
GLA_HEADS = 4
GLA_DK = 128
GLA_DV = 256
GLA_TAU = 16.0
GLA_CHUNK = 64
SB_HEADS = 16
SB_HEAD_DIM = 64
LANES = 128
SUBLANES = 8
MXU_DIM = 256
HEADS_PER_SLAB = LANES // SB_HEAD_DIM
VMEM_LIMIT = 48 * 1024 * 1024

COL_QA, COL_KA, COL_VA, COL_RA = 0, 512, 1024, 2048
COL_QB, COL_KB, COL_VB, COL_GA, COL_GB = 3072, 4096, 5120, 6144, 7168
PROJ_W = 8192


def _rms(x, g):
    return x * lax.rsqrt(jnp.mean(x * x, axis=-1, keepdims=True) + EPS) * g


def _const_spec(shape):
    return pl.BlockSpec(shape, lambda *_: (0,) * len(shape), pipeline_mode=pl.Buffered(1))


def _dot(a, b):
    return jnp.dot(a, b, preferred_element_type=F32)


def _dot_nt(a, b):
    return lax.dot_general(a, b, (((1,), (1,)), ((), ())), preferred_element_type=F32)


def _dot_tn(a, b):
    return lax.dot_general(a, b, (((0,), (0,)), ((), ())), preferred_element_type=F32)


def _split3(x):
    hi = x.astype(BF16)
    r1 = x - hi.astype(F32)
    mid = r1.astype(BF16)
    lo = (r1 - mid.astype(F32)).astype(BF16)
    return hi, mid, lo


def _split2(x):
    hi = x.astype(BF16)
    lo = (x - hi.astype(F32)).astype(BF16)
    return hi, lo


def _ffn_kernel(x_ref, gpre_ref, wg_ref, wu_ref, wd_ref, gpost_ref, *rest, ff_chunk, emit_u):
    if emit_u:
        gnext_ref, h_ref, u_ref, act_ref = rest
    else:
        h_ref, act_ref = rest
    x = x_ref[...]
    u = _rms(x, gpre_ref[...]).astype(BF16)
    d_ff = wg_ref.shape[1]
    for c in range(d_ff // ff_chunk):
        sl = slice(c * ff_chunk, (c + 1) * ff_chunk)
        gate = _dot(u, wg_ref[:, sl])
        up = _dot(u, wu_ref[:, sl])
        act_ref[:, sl] = (gate * jax.nn.sigmoid(gate) * up).astype(BF16)
    y = _dot(act_ref[...], wd_ref[...])
    h = x + 0.5 * _rms(y, gpost_ref[...])
    h_ref[...] = h
    if emit_u:
        u_ref[...] = _rms(h, gnext_ref[...]).astype(BF16)


def _ffn(x, g_pre, wg, wu, wd, g_post, g_next=None, *, tm):
    m, d = x.shape
    d_ff = wg.shape[1]
    emit_u = g_next is not None
    row = pl.BlockSpec((tm, d), lambda i: (i, 0))
    in_specs = [row, _const_spec((1, d)), _const_spec((d, d_ff)), _const_spec((d, d_ff)),
                _const_spec((d_ff, d)), _const_spec((1, d))]
    args = [x, g_pre, wg, wu, wd, g_post]
    out_shape = [jax.ShapeDtypeStruct((m, d), F32)]
    out_specs = [row]
    if emit_u:
        in_specs.append(_const_spec((1, d)))
        args.append(g_next)
        out_shape.append(jax.ShapeDtypeStruct((m, d), BF16))
        out_specs.append(row)
    outs = pl.pallas_call(
        functools.partial(_ffn_kernel, ff_chunk=MXU_DIM, emit_u=emit_u),
        grid=(m // tm,),
        in_specs=in_specs,
        out_specs=out_specs,
        out_shape=out_shape,
        scratch_shapes=[pltpu.VMEM((tm, d_ff), BF16)],
        compiler_params=pltpu.CompilerParams(
            dimension_semantics=("arbitrary",), vmem_limit_bytes=VMEM_LIMIT),
    )(*args)
    return outs if emit_u else outs[0]


def _proj_kernel(u_ref, w_ref, wgk1_ref, wgk2_ref, bgk_ref, p_ref, g_ref):
    u = u_ref[...]
    p_ref[...] = _dot(u, w_ref[...])

    @pl.when(pl.program_id(1) == 0)
    def _():
        low = _dot(u, wgk1_ref[...]).astype(BF16)
        zg = _dot(low, wgk2_ref[...]) + bgk_ref[...]
        g_ref[...] = (jnp.minimum(zg, 0.0) - jnp.log(1.0 + jnp.exp(-jnp.abs(zg)))) * (1.0 / GLA_TAU)


def _proj(u, w_main, w_gk1, w_gk2, b_gk, *, tm, tn):
    m, d = u.shape
    n = w_main.shape[1]
    qk = w_gk2.shape[1]
    return pl.pallas_call(
        _proj_kernel,
        grid=(m // tm, n // tn),
        in_specs=[pl.BlockSpec((tm, d), lambda i, j: (i, 0)),
                  pl.BlockSpec((d, tn), lambda i, j: (0, j)),
                  _const_spec(w_gk1.shape), _const_spec(w_gk2.shape), _const_spec((1, qk))],
        out_specs=[pl.BlockSpec((tm, tn), lambda i, j: (i, j)),
                   pl.BlockSpec((tm, qk), lambda i, j: (i, 0))],
        out_shape=[jax.ShapeDtypeStruct((m, n), F32), jax.ShapeDtypeStruct((m, qk), F32)],
        compiler_params=pltpu.CompilerParams(
            dimension_semantics=("arbitrary", "arbitrary"), vmem_limit_bytes=VMEM_LIMIT),
    )(u, w_main, w_gk1, w_gk2, b_gk)


def _gla_chunk(q, k, v, g, s, tri, ones_c, causal):
    parts = _split3(g)
    b = _dot(tri, parts[0]) + _dot(tri, parts[1]) + _dot(tri, parts[2])
    b_sum = _dot_tn(parts[0], ones_c) + _dot_tn(parts[1], ones_c) + _dot_tn(parts[2], ones_c)
    decay = jnp.exp(b_sum)
    c = q.shape[0]
    b_last = b[c - 1:c, :]
    q_in = (q * (GLA_DK ** -0.5) * jnp.exp(b)).astype(BF16)
    k_in = (k * jnp.exp(-b)).astype(BF16)
    k_end = (k * jnp.exp(b_last - b)).astype(BF16)
    vb = v.astype(BF16)
    a = jnp.where(causal, _dot_nt(q_in, k_in), 0.0)
    o = _dot(a.astype(BF16), vb) + _dot(q_in, s.astype(BF16))
    decay_full = jnp.concatenate([decay] * (GLA_DV // LANES), axis=1)
    s_new = decay_full * s + _dot_tn(k_end, vb)
    return o, s_new


def _gla_kernel(*refs, chunk, n_chunks, lead):
    if lead:
        km_ref, vm_ref, gm_ref = refs[:3]
        refs = refs[3:]
    else:
        s0_ref = refs[0]
        refs = refs[1:]
    q_ref, k_ref, v_ref, g_ref, r_ref, gn_ref, o_ref, sfin_ref, s_scr = refs
    step = pl.program_id(2)

    def consts(c):
        rows = lax.broadcasted_iota(jnp.int32, (c, c), 0)
        cols = lax.broadcasted_iota(jnp.int32, (c, c), 1)
        causal = cols <= rows
        return causal.astype(BF16), jnp.ones((c, LANES), BF16), causal

    @pl.when(step == 0)
    def _():
        if lead:
            tri, ones_c, causal = consts(lead)
            zeros = jnp.zeros((lead, GLA_DK), F32)
            _, s_init = _gla_chunk(zeros, km_ref[...], vm_ref[...], gm_ref[...],
                                   jnp.zeros((GLA_DK, GLA_DV), F32), tri, ones_c, causal)
            s_scr[...] = s_init
        else:
            s_scr[...] = s0_ref[...]

    tri, ones_c, causal = consts(chunk)
    s = s_scr[...]
    gn = gn_ref[...]
    for ci in range(n_chunks):
        sl = slice(ci * chunk, (ci + 1) * chunk)
        o, s = _gla_chunk(q_ref[sl, :], k_ref[sl, :], v_ref[sl, :], g_ref[sl, :], s, tri, ones_c, causal)
        r = r_ref[sl, :]
        o_ref[sl, :] = (_rms(o, gn) * (r * jax.nn.sigmoid(r))).astype(BF16)
    s_scr[...] = s

    @pl.when(step == pl.num_programs(2) - 1)
    def _():
        sfin_ref[...] = s


def _gla(proj, glog, gla_norm_g, *, n_seq, seq_len, rows_per_step, chunk, meta=None, state0=None):
    steps = seq_len // rows_per_step
    n_chunks = rows_per_step // chunk
    dk_blk = lambda col: col // GLA_DK
    dv_blk = lambda col: col // GLA_DV

    def rows(width, blk0):
        return pl.BlockSpec((rows_per_step, width), lambda b, h, s: (b * steps + s, blk0 + h))

    in_specs, args = [], []
    if meta is not None:
        pm, gm = meta
        in_specs += [pl.BlockSpec((N_META, GLA_DK), lambda b, h, s: (0, dk_blk(COL_KA) + h)),
                     pl.BlockSpec((N_META, GLA_DV), lambda b, h, s: (0, dv_blk(COL_VA) + h)),
                     pl.BlockSpec((N_META, GLA_DK), lambda b, h, s: (0, h))]
        args += [pm, pm, gm]
    else:
        in_specs.append(pl.BlockSpec((None, None, GLA_DK, GLA_DV), lambda b, h, s: (b, h, 0, 0)))
        args.append(state0)
    in_specs += [rows(GLA_DK, dk_blk(COL_QA)), rows(GLA_DK, dk_blk(COL_KA)), rows(GLA_DV, dv_blk(COL_VA)),
                 rows(GLA_DK, 0), rows(GLA_DV, dv_blk(COL_RA)), _const_spec((1, GLA_DV))]
    args += [proj, proj, proj, glog, proj, gla_norm_g]
    m = n_seq * seq_len
    return pl.pallas_call(
        functools.partial(_gla_kernel, chunk=chunk, n_chunks=n_chunks, lead=N_META if meta is not None else 0),
        grid=(n_seq, GLA_HEADS, steps),
        in_specs=in_specs,
        out_specs=[rows(GLA_DV, 0),
                   pl.BlockSpec((None, None, GLA_DK, GLA_DV), lambda b, h, s: (b, h, 0, 0))],
        out_shape=[jax.ShapeDtypeStruct((m, GLA_HEADS * GLA_DV), BF16),
                   jax.ShapeDtypeStruct((n_seq, GLA_HEADS, GLA_DK, GLA_DV), F32)],
        scratch_shapes=[pltpu.VMEM((GLA_DK, GLA_DV), F32)],
        compiler_params=pltpu.CompilerParams(
            dimension_semantics=("arbitrary", "arbitrary", "arbitrary"), vmem_limit_bytes=VMEM_LIMIT),
    )(*args)


def _sb_tile(z, run, upper, mask):
    l = -(jnp.maximum(z, 0.0) + jnp.log(1.0 + jnp.exp(-jnp.abs(z))))
    if mask is not None:
        l = jnp.where(mask, l, 0.0)
    hi, lo = _split2(l)
    later = _dot(hi, upper) + _dot(lo, upper)
    a = jnp.exp(z + l + later + run)
    if mask is not None:
        a = jnp.where(mask, a, 0.0)
    return a, run + later[:, 0:1] + l[:, 0:1]


def _strict_upper(t):
    rows = lax.broadcasted_iota(jnp.int32, (t, t), 0)
    cols = lax.broadcasted_iota(jnp.int32, (t, t), 1)
    return (rows > cols).astype(BF16)


def _sb_prompt_kernel(bias_ref, q_ref, k_ref, v_ref, km_ref, vm_ref, o_ref, *, tile):
    pair = pl.program_id(1)
    qi = pl.program_id(2)
    low_lanes = lax.broadcasted_iota(jnp.int32, (1, LANES), 1) < SB_HEAD_DIM
    q = q_ref[...] * (SB_HEAD_DIM ** -0.5)
    q_heads = (jnp.where(low_lanes, q, 0.0).astype(BF16), jnp.where(low_lanes, 0.0, q).astype(BF16))
    biases = (bias_ref[HEADS_PER_SLAB * pair], bias_ref[HEADS_PER_SLAB * pair + 1])

    def visit(kb, vb, runs, acc, upper, mask):
        kbf = kb.astype(BF16)
        v_heads = (jnp.where(low_lanes, vb, 0.0).astype(BF16), jnp.where(low_lanes, 0.0, vb).astype(BF16))
        new_runs = []
        for qh, bh, vh, run in zip(q_heads, biases, v_heads, runs):
            a, run = _sb_tile(_dot_nt(qh, kbf) + bh, run, upper, mask)
            acc = acc + _dot(a.astype(BF16), vh)
            new_runs.append(run)
        return tuple(new_runs), acc

    upper = _strict_upper(tile)
    rows = lax.broadcasted_iota(jnp.int32, (tile, tile), 0)
    cols = lax.broadcasted_iota(jnp.int32, (tile, tile), 1)
    zero_run = jnp.zeros((tile, 1), F32)
    start = pl.multiple_of(qi * tile, tile)
    runs, acc = visit(k_ref[pl.ds(start, tile), :], v_ref[pl.ds(start, tile), :],
                      (zero_run, zero_run), jnp.zeros((tile, LANES), F32), upper, cols < rows)

    def body(jj, carry):
        runs, acc = carry
        off = pl.multiple_of((qi - 1 - jj) * tile, tile)
        return visit(k_ref[pl.ds(off, tile), :], v_ref[pl.ds(off, tile), :], runs, acc, upper, None)

    runs, acc = lax.fori_loop(0, qi, body, (runs, acc))

    meta_cols = lax.broadcasted_iota(jnp.int32, (tile, LANES), 1)
    _, acc = visit(km_ref[...], vm_ref[...], runs, acc, _strict_upper(LANES), meta_cols < N_META)
    o_ref[...] = acc.astype(BF16)


def _sb_prompt(proj, proj_meta, sb_bias, *, n_seq, seq_len, tile):
    p3 = proj.reshape(n_seq, seq_len, PROJ_W)
    slab = lambda col: col // LANES
    return pl.pallas_call(
        functools.partial(_sb_prompt_kernel, tile=tile),
        grid=(n_seq, SB_HEADS // HEADS_PER_SLAB, seq_len // tile),
        in_specs=[pl.BlockSpec(memory_space=pltpu.SMEM),
                  pl.BlockSpec((None, tile, LANES), lambda b, p, i: (b, i, slab(COL_QB) + p)),
                  pl.BlockSpec((None, seq_len, LANES), lambda b, p, i: (b, 0, slab(COL_KB) + p)),
                  pl.BlockSpec((None, seq_len, LANES), lambda b, p, i: (b, 0, slab(COL_VB) + p)),
                  pl.BlockSpec((LANES, LANES), lambda b, p, i: (0, slab(COL_KB) + p)),
                  pl.BlockSpec((LANES, LANES), lambda b, p, i: (0, slab(COL_VB) + p))],
        out_specs=pl.BlockSpec((None, tile, LANES), lambda b, p, i: (b, i, p)),
        out_shape=jax.ShapeDtypeStruct((n_seq, seq_len, SB_HEADS * SB_HEAD_DIM), BF16),
        compiler_params=pltpu.CompilerParams(
            dimension_semantics=("arbitrary", "arbitrary", "arbitrary"), vmem_limit_bytes=VMEM_LIMIT),
    )(sb_bias, p3, p3, p3, proj_meta, proj_meta).reshape(n_seq * seq_len, SB_HEADS * SB_HEAD_DIM)


def _sb_sample_kernel(pt_ref, bias_ref, q_ref, kn_ref, vn_ref, *rest, n_tok, pages_per_step):
    k_refs = rest[:pages_per_step]
    v_refs = rest[pages_per_step:2 * pages_per_step]
    o_ref, qbd_scr, acc_scr, run_scr = rest[2 * pages_per_step:]
    step = pl.program_id(1)
    n_rows = n_tok * SB_HEADS
    width = SB_HEADS * SB_HEAD_DIM
    row_id = lax.broadcasted_iota(jnp.int32, (n_rows, width), 0)
    col_id = lax.broadcasted_iota(jnp.int32, (n_rows, width), 1)
    own_head = lax.shift_right_logical(col_id, SB_HEAD_DIM.bit_length() - 1) == (row_id & (SB_HEADS - 1))
    bias = bias_ref[...]

    @pl.when(step == 0)
    def _():
        q = q_ref[...] * (SB_HEAD_DIM ** -0.5)
        q_rows = jnp.concatenate(
            [jnp.broadcast_to(q[t:t + 1, :], (SB_HEADS, width)) for t in range(n_tok)], axis=0)
        qbd = jnp.where(own_head, q_rows, 0.0).astype(BF16)
        qbd_scr[...] = qbd
        qf = qbd.astype(F32)
        tok_of_row = lax.shift_right_logical(lax.broadcasted_iota(jnp.int32, (n_rows, 1), 0),
                                             SB_HEADS.bit_length() - 1)
        run = jnp.zeros((n_rows, 1), F32)
        acc = jnp.zeros((n_rows, width), F32)
        for j in reversed(range(n_tok)):
            kj = kn_ref[j:j + 1, :].astype(BF16).astype(F32)
            vj = vn_ref[j:j + 1, :].astype(BF16).astype(F32)
            z = jnp.sum(qf * kj, axis=-1, keepdims=True) + bias
            l = -(jnp.maximum(z, 0.0) + jnp.log(1.0 + jnp.exp(-jnp.abs(z))))
            seen = tok_of_row > j
            a = jnp.where(seen, jnp.exp(z + l + run), 0.0)
            run = run + jnp.where(seen, l, 0.0)
            acc = acc + a * vj
        acc_scr[...] = acc
        run_scr[...] = run

    upper = _strict_upper(k_refs[0].shape[1])
    qbd = qbd_scr[...]
    run = run_scr[...]
    acc = acc_scr[...]
    for k_ref, v_ref in zip(k_refs, v_refs):
        z = _dot(qbd, k_ref[...].astype(BF16)) + bias
        a, run = _sb_tile(z, run, upper, None)
        acc = acc + _dot_nt(a.astype(BF16), v_ref[...].astype(BF16))
    acc_scr[...] = acc
    run_scr[...] = run

    @pl.when(step == pl.num_programs(1) - 1)
    def _():
        picked = jnp.where(own_head, acc, 0.0)
        for t in range(n_tok):
            o_ref[t:t + 1, :] = jnp.sum(picked[t * SB_HEADS:(t + 1) * SB_HEADS, :], axis=0, keepdims=True)


def _sb_sample(proj, cache_k, cache_v, page_table, sb_bias, *, n_seq, n_tok, pages_per_step):
    width = SB_HEADS * SB_HEAD_DIM
    n_pages = page_table.shape[1]
    page_size = cache_k.shape[1]
    steps = n_pages // pages_per_step
    p3 = proj.reshape(n_seq, n_tok, PROJ_W)
    ck = cache_k.transpose(0, 2, 3, 1).reshape(cache_k.shape[0], width, page_size)
    cv = cache_v.transpose(0, 2, 3, 1).reshape(cache_v.shape[0], width, page_size)
    bias_rows = jnp.tile(sb_bias, n_tok).reshape(n_tok * SB_HEADS, 1)
    blk = lambda col: col // width

    def new_rows(col):
        return pl.BlockSpec((None, n_tok, width), lambda n, s, pt: (n, 0, blk(col)))

    def page(u):
        return pl.BlockSpec((None, width, page_size),
                            lambda n, s, pt: (pt[n, n_pages - 1 - (s * pages_per_step + u)], 0, 0))

    n_rows = n_tok * SB_HEADS
    grid_spec = pltpu.PrefetchScalarGridSpec(
        num_scalar_prefetch=1,
        grid=(n_seq, steps),
        in_specs=[pl.BlockSpec((n_rows, 1), lambda n, s, pt: (0, 0)),
                  new_rows(COL_QB), new_rows(COL_KB), new_rows(COL_VB)]
                 + [page(u) for u in range(pages_per_step)] + [page(u) for u in range(pages_per_step)],
        out_specs=pl.BlockSpec((None, n_tok, width), lambda n, s, pt: (n, 0, 0)),
        scratch_shapes=[pltpu.VMEM((n_rows, width), BF16), pltpu.VMEM((n_rows, width), F32),
                        pltpu.VMEM((n_rows, 1), F32)],
    )
    out = pl.pallas_call(
        functools.partial(_sb_sample_kernel, n_tok=n_tok, pages_per_step=pages_per_step),
        grid_spec=grid_spec,
        out_shape=jax.ShapeDtypeStruct((n_seq, n_tok, width), F32),
        compiler_params=pltpu.CompilerParams(
            dimension_semantics=("arbitrary", "arbitrary"), vmem_limit_bytes=VMEM_LIMIT),
    )(page_table, bias_rows, p3, p3, p3, *([ck] * pages_per_step), *([cv] * pages_per_step))
    return out.reshape(n_seq * n_tok, width)


def _merge_kernel(oa_ref, ob_ref, ga_ref, gb_ref, h_ref, woa_ref, wob_ref, wout_ref, g_ref, out_ref):
    pa = _dot(oa_ref[...].astype(BF16), woa_ref[...])
    pb = _dot(ob_ref[...].astype(BF16), wob_ref[...])
    mixed = jax.nn.sigmoid(ga_ref[...]) * pa + jax.nn.sigmoid(gb_ref[...]) * pb
    out_ref[...] = h_ref[...] + _rms(_dot(mixed.astype(BF16), wout_ref[...]), g_ref[...])


def _merge(o_a, o_b, proj, h, w_o_gla, w_o_sb, w_out, g_post, *, tm):
    m, d = h.shape
    row = pl.BlockSpec((tm, d), lambda i: (i, 0))
    return pl.pallas_call(
        _merge_kernel,
        grid=(m // tm,),
        in_specs=[row, row,
                  pl.BlockSpec((tm, d), lambda i: (i, COL_GA // d)),
                  pl.BlockSpec((tm, d), lambda i: (i, COL_GB // d)),
                  row, _const_spec(w_o_gla.shape), _const_spec(w_o_sb.shape), _const_spec(w_out.shape),
                  _const_spec((1, d))],
        out_specs=row,
        out_shape=jax.ShapeDtypeStruct((m, d), F32),
        compiler_params=pltpu.CompilerParams(
            dimension_semantics=("arbitrary",), vmem_limit_bytes=VMEM_LIMIT),
    )(o_a, o_b, proj, proj, h, w_o_gla, w_o_sb, w_out, g_post)


def _row_tile(m, cap):
    return m if m <= cap else cap


def kernel(x_prompt, x_sample, cache_k, cache_v, state_gla, page_table, meta_tokens,
           ffn1_pre_g, ffn1_w_gate, ffn1_w_up, ffn1_w_down, ffn1_post_g,
           mix_pre_g, w_in, w_gk2, b_gk, gla_norm_g, sb_bias, w_o_gla, w_o_sb, w_out, mix_post_g,
           ffn2_pre_g, ffn2_w_gate, ffn2_w_up, ffn2_w_down, ffn2_post_g):
    n_seq, seq_len, d = x_prompt.shape
    n_dec, n_tok, _ = x_sample.shape
    assert ffn1_pre_g.shape[0] == 1, "single layer"
    qk = GLA_HEADS * GLA_DK
    v_w = GLA_HEADS * GLA_DV
    rank = w_gk2.shape[1]

    bf = lambda w: w[0].astype(BF16)
    w1g, w1u, w1d = bf(ffn1_w_gate), bf(ffn1_w_up), bf(ffn1_w_down)
    w2g, w2u, w2d = bf(ffn2_w_gate), bf(ffn2_w_up), bf(ffn2_w_down)
    w_in0 = w_in[0]
    gk0 = 2 * qk + v_w
    w_main = jnp.concatenate([w_in0[:, :gk0], w_in0[:, gk0 + rank:]], axis=1).astype(BF16)
    w_gk1 = jnp.pad(w_in0[:, gk0:gk0 + rank], ((0, 0), (0, LANES - rank))).astype(BF16)
    w_gk2p = jnp.pad(w_gk2[0], ((0, LANES - rank), (0, 0))).astype(BF16)
    woa, wob, wout = bf(w_o_gla), bf(w_o_sb), bf(w_out)
    row = lambda g: g[0].reshape(1, -1)
    bias = sb_bias[0]

    def pre_mix(x, tm):
        h, u = _ffn(x, row(ffn1_pre_g), w1g, w1u, w1d, row(ffn1_post_g), row(mix_pre_g), tm=tm)
        proj, glog = _proj(u, w_main, w_gk1, w_gk2p, row(b_gk), tm=tm, tn=min(1024, PROJ_W))
        return h, proj, glog

    def post_mix(h, proj, o_a, o_b, tm):
        h2 = _merge(o_a, o_b, proj, h, woa, wob, wout, row(mix_post_g), tm=tm)
        return _ffn(h2, row(ffn2_pre_g), w2g, w2u, w2d, row(ffn2_post_g), tm=tm)

    x_meta = jnp.pad(meta_tokens.astype(F32), ((0, LANES - N_META), (0, 0)))
    _, proj_m, glog_m = pre_mix(x_meta, LANES)

    m_p = n_seq * seq_len
    tm_p = _row_tile(m_p, 512)
    h_p, proj_p, glog_p = pre_mix(x_prompt.reshape(m_p, d), tm_p)
    oa_p, s_p = _gla(proj_p, glog_p, row(gla_norm_g), n_seq=n_seq, seq_len=seq_len,
                     rows_per_step=_row_tile(seq_len, 512), chunk=GLA_CHUNK, meta=(proj_m, glog_m))
    ob_p = _sb_prompt(proj_p, proj_m, bias, n_seq=n_seq, seq_len=seq_len, tile=MXU_DIM)
    y_p = post_mix(h_p, proj_p, oa_p, ob_p, tm_p)

    m_s = n_dec * n_tok
    tm_s = _row_tile(m_s, 512)
    h_s, proj_s, glog_s = pre_mix(x_sample.reshape(m_s, d), tm_s)
    pad_tok = lambda a: jnp.pad(a.reshape(n_dec, n_tok, -1), ((0, 0), (0, SUBLANES - n_tok), (0, 0))
                                ).reshape(n_dec * SUBLANES, -1)
    oa_s, s_s = _gla(pad_tok(proj_s[:, :COL_QB]), pad_tok(glog_s), row(gla_norm_g), n_seq=n_dec, seq_len=SUBLANES,
                     rows_per_step=SUBLANES, chunk=SUBLANES, state0=state_gla[0])
    oa_s = oa_s.reshape(n_dec, SUBLANES, v_w)[:, :n_tok].reshape(m_s, v_w)
    ob_s = _sb_sample(proj_s, cache_k[0], cache_v[0], page_table, bias,
                      n_seq=n_dec, n_tok=n_tok, pages_per_step=4)
    y_s = post_mix(h_s, proj_s, oa_s, ob_s, tm_s)

    heads = (SB_HEADS, SB_HEAD_DIM)
    w_sb = SB_HEADS * SB_HEAD_DIM

    def with_meta(col):
        meta_rows = jnp.broadcast_to(proj_m[:N_META, col:col + w_sb][None], (n_seq, N_META, w_sb))
        main = proj_p[:, col:col + w_sb].reshape(n_seq, seq_len, w_sb)
        return jnp.concatenate([meta_rows, main], axis=1).reshape(1, n_seq, N_META + seq_len, *heads)

    return (y_p.reshape(n_seq, seq_len, d),
            y_s.reshape(n_dec, n_tok, d),
            with_meta(COL_KB), with_meta(COL_VB),
            s_p[None],
            proj_s[:, COL_KB:COL_KB + w_sb].reshape(1, n_dec, n_tok, *heads),
            proj_s[:, COL_VB:COL_VB + w_sb].reshape(1, n_dec, n_tok, *heads),
            s_s[None])
```

```python
import functools

import jax
import jax.numpy as jnp
from jax import lax
from jax.experimental import pallas as pl
from jax.experimental.pallas import tpu as pltpu

F32 = jnp.float32
BF16 = jnp.bfloat16

EPS = 1e-6
N_META = 16
GLA_HEADS = 4
GLA_DK = 128
GLA_DV = 256
GLA_TAU = 16.0
GLA_CHUNK = 64
SB_HEADS = 16
SB_HEAD_DIM = 64
LANES = 128
SUBLANES = 8
MXU_DIM = 256
HEADS_PER_SLAB = LANES // SB_HEAD_DIM
VMEM_LIMIT = 48 * 1024 * 1024
NEG_BIG = -1e30

COL_QA, COL_KA, COL_VA, COL_RA = 0, 512, 1024, 2048
COL_QB, COL_KB, COL_VB, COL_GA, COL_GB = 3072, 4096, 5120, 6144, 7168
PROJ_W = 8192


def _rms(x, g):
    return x * lax.rsqrt(jnp.mean(x * x, axis=-1, keepdims=True) + EPS) * g


def _const_spec(shape):
    return pl.BlockSpec(shape, lambda *_: (0,) * len(shape), pipeline_mode=pl.Buffered(1))


def _dot(a, b):
    return jnp.dot(a, b, preferred_element_type=F32)


def _dot_nt(a, b):
    return lax.dot_general(a, b, (((1,), (1,)), ((), ())), preferred_element_type=F32)


def _dot_tn(a, b):
    return lax.dot_general(a, b, (((0,), (0,)), ((), ())), preferred_element_type=F32)


def _split3(x):
    hi = x.astype(BF16)
    r1 = x - hi.astype(F32)
    mid = r1.astype(BF16)
    lo = (r1 - mid.astype(F32)).astype(BF16)
    return hi, mid, lo


def _split2(x):
    hi = x.astype(BF16)
    lo = (x - hi.astype(F32)).astype(BF16)
    return hi, lo


def _ffn_kernel(x_ref, gpre_ref, wg_ref, wu_ref, wd_ref, gpost_ref, *rest, ff_chunk, emit_u):
    if emit_u:
        gnext_ref, h_ref, u_ref, act_ref = rest
    else:
        h_ref, act_ref = rest
    x = x_ref[...]
    u = _rms(x, gpre_ref[...]).astype(BF16)
    d_ff = wg_ref.shape[1]
    for c in range(d_ff // ff_chunk):
        sl = slice(c * ff_chunk, (c + 1) * ff_chunk)
        gate = _dot(u, wg_ref[:, sl])
        up = _dot(u, wu_ref[:, sl])
        act_ref[:, sl] = (gate * jax.nn.sigmoid(gate) * up).astype(BF16)
    y = _dot(act_ref[...], wd_ref[...])
    h = x + 0.5 * _rms(y, gpost_ref[...])
    h_ref[...] = h
    if emit_u:
        u_ref[...] = _rms(h, gnext_ref[...]).astype(BF16)


def _ffn(x, g_pre, wg, wu, wd, g_post, g_next=None, *, tm):
    m, d = x.shape
    d_ff = wg.shape[1]
    emit_u = g_next is not None
    row = pl.BlockSpec((tm, d), lambda i: (i, 0))
    in_specs = [row, _const_spec((1, d)), _const_spec((d, d_ff)), _const_spec((d, d_ff)),
                _const_spec((d_ff, d)), _const_spec((1, d))]
    args = [x, g_pre, wg, wu, wd, g_post]
    out_shape = [jax.ShapeDtypeStruct((m, d), F32)]
    out_specs = [row]
    if emit_u:
        in_specs.append(_const_spec((1, d)))
        args.append(g_next)
        out_shape.append(jax.ShapeDtypeStruct((m, d), BF16))
        out_specs.append(row)
    outs = pl.pallas_call(
        functools.partial(_ffn_kernel, ff_chunk=MXU_DIM, emit_u=emit_u),
        grid=(m // tm,),
        in_specs=in_specs,
        out_specs=out_specs,
        out_shape=out_shape,
        scratch_shapes=[pltpu.VMEM((tm, d_ff), BF16)],
        compiler_params=pltpu.CompilerParams(
            dimension_semantics=("arbitrary",), vmem_limit_bytes=VMEM_LIMIT),
    )(*args)
    return outs if emit_u else outs[0]


def _proj_kernel(u_ref, w_ref, wgk1_ref, wgk2_ref, bgk_ref, p_ref, g_ref):
    u = u_ref[...]
    p_ref[...] = _dot(u, w_ref[...])

    @pl.when(pl.program_id(1) == 0)
    def _():
        low = _dot(u, wgk1_ref[...]).astype(BF16)
        zg = _dot(low, wgk2_ref[...]) + bgk_ref[...]
        g_ref[...] = (jnp.minimum(zg, 0.0) - jnp.log(1.0 + jnp.exp(-jnp.abs(zg)))) * (1.0 / GLA_TAU)


def _proj(u, w_main, w_gk1, w_gk2, b_gk, *, tm, tn):
    m, d = u.shape
    n = w_main.shape[1]
    qk = w_gk2.shape[1]
    return pl.pallas_call(
        _proj_kernel,
        grid=(m // tm, n // tn),
        in_specs=[pl.BlockSpec((tm, d), lambda i, j: (i, 0)),
                  pl.BlockSpec((d, tn), lambda i, j: (0, j)),
                  _const_spec(w_gk1.shape), _const_spec(w_gk2.shape), _const_spec((1, qk))],
        out_specs=[pl.BlockSpec((tm, tn), lambda i, j: (i, j)),
                   pl.BlockSpec((tm, qk), lambda i, j: (i, 0))],
        out_shape=[jax.ShapeDtypeStruct((m, n), F32), jax.ShapeDtypeStruct((m, qk), F32)],
        compiler_params=pltpu.CompilerParams(
            dimension_semantics=("arbitrary", "arbitrary"), vmem_limit_bytes=VMEM_LIMIT),
    )(u, w_main, w_gk1, w_gk2, b_gk)


def _kv_t_kernel(wk_ref, wv_ref, u_ref, k_ref, v_ref):
    u = u_ref[...]
    k_ref[...] = _dot_nt(wk_ref[...], u)
    v_ref[...] = _dot_nt(wv_ref[...], u)


def _kv_t(wk_t, wv_t, u_full, *, tp):
    width, d = wk_t.shape
    n_seq, n_pos, _ = u_full.shape
    out = pl.BlockSpec((None, width, tp), lambda b, j: (b, 0, j))
    return pl.pallas_call(
        _kv_t_kernel,
        grid=(n_seq, pl.cdiv(n_pos, tp)),
        in_specs=[_const_spec((width, d)), _const_spec((width, d)),
                  pl.BlockSpec((None, tp, d), lambda b, j: (b, j, 0))],
        out_specs=[out, out],
        out_shape=[jax.ShapeDtypeStruct((n_seq, width, n_pos), F32)] * 2,
        compiler_params=pltpu.CompilerParams(
            dimension_semantics=("arbitrary", "arbitrary"), vmem_limit_bytes=VMEM_LIMIT),
    )(wk_t, wv_t, u_full)


def _gla_local(problems, c):
    rows = lax.broadcasted_iota(jnp.int32, (c, c), 0)
    cols = lax.broadcasted_iota(jnp.int32, (c, c), 1)
    causal = cols <= rows
    tri = causal.astype(BF16)
    ones_c = jnp.ones((c, LANES), BF16)
    parts = [_split3(g) for _, _, _, g in problems]
    cums = [_dot(tri, p[0]) + _dot(tri, p[1]) + _dot(tri, p[2]) for p in parts]
    sums = [_dot_tn(p[0], ones_c) + _dot_tn(p[1], ones_c) + _dot_tn(p[2], ones_c) for p in parts]
    q_ins, k_ins, k_ends, vbs = [], [], [], []
    for (q, k, v, _), b in zip(problems, cums):
        q_ins.append((q * (GLA_DK ** -0.5) * jnp.exp(b)).astype(BF16))
        k_ins.append((k * jnp.exp(-b)).astype(BF16))
        k_ends.append((k * jnp.exp(b[c - 1:c, :] - b)).astype(BF16))
        vbs.append(v.astype(BF16))
    scores = [_dot_nt(qi, ki) for qi, ki in zip(q_ins, k_ins)]
    o_intra = [_dot(jnp.where(causal, a, 0.0).astype(BF16), vb) for a, vb in zip(scores, vbs)]
    d_state = [_dot_tn(ke, vb) for ke, vb in zip(k_ends, vbs)]
    decays = [jnp.concatenate([jnp.exp(bs)] * (GLA_DV // LANES), axis=1) for bs in sums]
    return list(zip(q_ins, o_intra, d_state, decays))


def _gla_kernel(*refs, n_heads, chunk, n_chunks, lead):
    if lead:
        km_ref, vm_ref, gm_ref = refs[:3]
        refs = refs[3:]
    else:
        s0_ref = refs[0]
        refs = refs[1:]
    q_ref, k_ref, v_ref, g_ref, r_ref, gn_ref, o_ref, sfin_ref, s_scr = refs
    step = pl.program_id(2)

    @pl.when(step == 0)
    def _():
        if lead:
            zeros = jnp.zeros((lead, GLA_DK), F32)
            (_, _, d_state, _), = _gla_local([(zeros, km_ref[...], vm_ref[...], gm_ref[...])], lead)
            s_scr[0] = d_state
        else:
            s_scr[...] = s0_ref[...]

    dk = lambda h: slice(h * GLA_DK, (h + 1) * GLA_DK)
    dv = lambda h: slice(h * GLA_DV, (h + 1) * GLA_DV)
    problems = []
    for h in range(n_heads):
        for ci in range(n_chunks):
            sl = slice(ci * chunk, (ci + 1) * chunk)
            problems.append((q_ref[sl, dk(h)], k_ref[sl, dk(h)], v_ref[sl, dv(h)], g_ref[sl, dk(h)]))
    local = _gla_local(problems, chunk)
    gn = gn_ref[...]
    for h in range(n_heads):
        s = s_scr[h]
        for ci in range(n_chunks):
            sl = slice(ci * chunk, (ci + 1) * chunk)
            q_in, o_intra, d_state, decay = local[h * n_chunks + ci]
            o = o_intra + _dot(q_in, s.astype(BF16))
            s = decay * s + d_state
            r = r_ref[sl, dv(h)]
            o_ref[sl, dv(h)] = (_rms(o, gn) * (r * jax.nn.sigmoid(r))).astype(BF16)
        s_scr[h] = s

    @pl.when(step == pl.num_programs(2) - 1)
    def _():
        sfin_ref[...] = s_scr[...]


def _gla(proj, glog, gla_norm_g, *, n_seq, seq_len, rows_per_step, chunk, n_heads, meta=None, state0=None):
    steps = seq_len // rows_per_step
    n_chunks = rows_per_step // chunk
    groups = GLA_HEADS // n_heads
    wk, wv = n_heads * GLA_DK, n_heads * GLA_DV

    def rows(width, col0):
        return pl.BlockSpec((rows_per_step, width), lambda b, h, s: (b * steps + s, col0 // width + h))

    state_spec = pl.BlockSpec((None, n_heads, GLA_DK, GLA_DV), lambda b, h, s: (b, h, 0, 0))
    in_specs, args = [], []
    if meta is not None:
        assert n_heads == 1
        pm, gm = meta
        in_specs += [pl.BlockSpec((N_META, GLA_DK), lambda b, h, s: (0, COL_KA // GLA_DK + h)),
                     pl.BlockSpec((N_META, GLA_DV), lambda b, h, s: (0, COL_VA // GLA_DV + h)),
                     pl.BlockSpec((N_META, GLA_DK), lambda b, h, s: (0, h))]
        args += [pm, pm, gm]
    else:
        in_specs.append(state_spec)
        args.append(state0)
    in_specs += [rows(wk, COL_QA), rows(wk, COL_KA), rows(wv, COL_VA), rows(wk, 0), rows(wv, COL_RA),
                 _const_spec((1, GLA_DV))]
    args += [proj, proj, proj, glog, proj, gla_norm_g]
    m = n_seq * seq_len
    return pl.pallas_call(
        functools.partial(_gla_kernel, n_heads=n_heads, chunk=chunk, n_chunks=n_chunks,
                          lead=N_META if meta is not None else 0),
        grid=(n_seq, groups, steps),
        in_specs=in_specs,
        out_specs=[rows(wv, 0), state_spec],
        out_shape=[jax.ShapeDtypeStruct((m, GLA_HEADS * GLA_DV), BF16),
                   jax.ShapeDtypeStruct((n_seq, GLA_HEADS, GLA_DK, GLA_DV), F32)],
        scratch_shapes=[pltpu.VMEM((n_heads, GLA_DK, GLA_DV), F32)],
        compiler_params=pltpu.CompilerParams(
            dimension_semantics=("arbitrary", "arbitrary", "arbitrary"), vmem_limit_bytes=VMEM_LIMIT),
    )(*args)


def _sb_logs(zb):
    mx = jnp.maximum(zb, 0.0)
    mn = zb - mx
    lg = jnp.log(1.0 + jnp.exp(mn - mx))
    return mn - lg, lg + mx


def _suffix_matrix(t):
    rows = lax.broadcasted_iota(jnp.int32, (t, t), 0)
    cols = lax.broadcasted_iota(jnp.int32, (t, t), 1)
    upper = (rows > cols).astype(BF16)
    return jnp.concatenate([upper, upper], axis=0)


def _sb_suffix(nl, suffix):
    hi, lo = _split2(nl)
    later = _dot(jnp.concatenate([hi, lo], axis=1), suffix)
    return later, later[:, 0:1] + nl[:, 0:1]


def _sb_prompt_kernel(bias_ref, q_ref, k_ref, v_ref, km_ref, vm_ref, o_ref,
                      k_scr, vv_scr, bt_scr, sfx_scr, z_scr, w_scr, later_scr, tot_scr, nrun_scr, acc_scr,
                      *, tile, n_main):
    pair = pl.program_id(1)
    qi = pl.program_id(2)
    low_lanes = lax.broadcasted_iota(jnp.int32, (1, LANES), 1) < SB_HEAD_DIM
    both = lambda x: jnp.concatenate([jnp.where(low_lanes, x, 0.0), jnp.where(low_lanes, 0.0, x)], axis=0)
    DIAG, FULL, META, NONE = range(4)

    @pl.when(qi == 0)
    def _():
        for j in range(n_main):
            sl = slice(j * tile, (j + 1) * tile)
            k_scr[j] = k_ref[sl, :].astype(BF16)
            vv_scr[j] = both(v_ref[sl, :]).astype(BF16)
        pad = jnp.zeros((tile - LANES, LANES), F32)
        k_scr[n_main] = jnp.concatenate([km_ref[...], pad], axis=0).astype(BF16)
        vv_scr[n_main] = both(jnp.concatenate([vm_ref[...], pad], axis=0)).astype(BF16)
        rows = lax.broadcasted_iota(jnp.int32, (tile, tile), 0)
        cols = lax.broadcasted_iota(jnp.int32, (tile, tile), 1)
        kinds = {DIAG: jnp.where(cols < rows, 0.0, NEG_BIG), FULL: jnp.zeros((tile, tile), F32),
                 META: jnp.where(cols < N_META, 0.0, NEG_BIG), NONE: jnp.full((tile, tile), NEG_BIG, F32)}
        for kind, mask in kinds.items():
            for h in range(HEADS_PER_SLAB):
                bt_scr[kind, h * tile:(h + 1) * tile, :] = mask + bias_ref[HEADS_PER_SLAB * pair + h]
        sfx_scr[...] = _suffix_matrix(tile)

    n_tiles = qi + 2

    def key_slot(t):
        return jnp.where(t <= qi, qi - t, n_main)

    def kind_of(t):
        return jnp.where(t == 0, DIAG, jnp.where(t <= qi, FULL, jnp.where(t == qi + 1, META, NONE)))

    qq = both(q_ref[...] * (SB_HEAD_DIM ** -0.5)).astype(BF16)

    def logits(t):
        return _dot_nt(qq, k_scr[key_slot(jnp.minimum(t, n_tiles - 1))])

    def stage(t, cur, nxt):
        z_scr[nxt] = logits(t + 1)
        w, nl = _sb_logs(z_scr[cur] + bt_scr[kind_of(t)])
        later, tot = _sb_suffix(nl, sfx_scr[...])
        nrun = nrun_scr[...]
        a = jnp.exp(w_scr[nxt] - later_scr[nxt] - jnp.concatenate([nrun, nrun], axis=1)).astype(BF16)
        prev_slot = key_slot(jnp.clip(t - 1, 0, n_tiles - 1))
        acc_scr[...] += _dot(jnp.concatenate([a[:tile], a[tile:]], axis=1), vv_scr[prev_slot])
        nrun_scr[...] = nrun + tot_scr[nxt]
        w_scr[cur] = w
        later_scr[cur] = later
        tot_scr[cur] = jnp.broadcast_to(tot, tot_scr.shape[1:])

    z_scr[0] = logits(0)
    w_scr[1] = jnp.full(w_scr.shape[1:], NEG_BIG, F32)
    later_scr[1] = jnp.zeros(later_scr.shape[1:], F32)
    tot_scr[1] = jnp.zeros(tot_scr.shape[1:], F32)
    nrun_scr[...] = jnp.zeros(nrun_scr.shape, F32)
    acc_scr[...] = jnp.zeros(acc_scr.shape, F32)

    def body(u, carry):
        stage(2 * u, 0, 1)
        stage(2 * u + 1, 1, 0)
        return carry

    lax.fori_loop(0, (n_tiles + 2) // 2, body, 0)
    o_ref[...] = acc_scr[...].astype(BF16)


def _sb_prompt(proj, proj_meta, sb_bias, *, n_seq, seq_len, tile):
    p3 = proj.reshape(n_seq, seq_len, PROJ_W)
    slab = lambda col: col // LANES
    n_main = seq_len // tile
    pair_rows = HEADS_PER_SLAB * tile
    return pl.pallas_call(
        functools.partial(_sb_prompt_kernel, tile=tile, n_main=n_main),
        grid=(n_seq, SB_HEADS // HEADS_PER_SLAB, n_main),
        in_specs=[pl.BlockSpec(memory_space=pltpu.SMEM),
                  pl.BlockSpec((None, tile, LANES), lambda b, p, i: (b, i, slab(COL_QB) + p)),
                  pl.BlockSpec((None, seq_len, LANES), lambda b, p, i: (b, 0, slab(COL_KB) + p)),
                  pl.BlockSpec((None, seq_len, LANES), lambda b, p, i: (b, 0, slab(COL_VB) + p)),
                  pl.BlockSpec((LANES, LANES), lambda b, p, i: (0, slab(COL_KB) + p)),
                  pl.BlockSpec((LANES, LANES), lambda b, p, i: (0, slab(COL_VB) + p))],
        out_specs=pl.BlockSpec((None, tile, LANES), lambda b, p, i: (b, i, p)),
        out_shape=jax.ShapeDtypeStruct((n_seq, seq_len, SB_HEADS * SB_HEAD_DIM), BF16),
        scratch_shapes=[pltpu.VMEM((n_main + 1, tile, LANES), BF16),
                        pltpu.VMEM((n_main + 1, pair_rows, LANES), BF16),
                        pltpu.VMEM((4, pair_rows, tile), F32),
                        pltpu.VMEM((2 * tile, tile), BF16),
                        pltpu.VMEM((2, pair_rows, tile), F32),
                        pltpu.VMEM((2, pair_rows, tile), F32),
                        pltpu.VMEM((2, pair_rows, tile), F32),
                        pltpu.VMEM((2, pair_rows, LANES), F32),
                        pltpu.VMEM((pair_rows, LANES), F32),
                        pltpu.VMEM((tile, LANES), F32)],
        compiler_params=pltpu.CompilerParams(
            dimension_semantics=("arbitrary", "arbitrary", "arbitrary"), vmem_limit_bytes=VMEM_LIMIT),
    )(sb_bias, p3, p3, p3, proj_meta, proj_meta).reshape(n_seq * seq_len, SB_HEADS * SB_HEAD_DIM)


def _sb_sample_kernel(pt_ref, bias_ref, q_ref, kn_ref, vn_ref, *rest, n_tok, pages_per_step):
    k_refs = rest[:pages_per_step]
    v_refs = rest[pages_per_step:2 * pages_per_step]
    o_ref, qbd_scr, acc_scr, run_scr = rest[2 * pages_per_step:]
    step = pl.program_id(1)
    n_rows = n_tok * SB_HEADS
    width = SB_HEADS * SB_HEAD_DIM
    row_id = lax.broadcasted_iota(jnp.int32, (n_rows, width), 0)
    col_id = lax.broadcasted_iota(jnp.int32, (n_rows, width), 1)
    own_head = lax.shift_right_logical(col_id, SB_HEAD_DIM.bit_length() - 1) == (row_id & (SB_HEADS - 1))
    bias = bias_ref[...]

    @pl.when(step == 0)
    def _():
        q = q_ref[...] * (SB_HEAD_DIM ** -0.5)
        q_rows = jnp.concatenate(
            [jnp.broadcast_to(q[t:t + 1, :], (SB_HEADS, width)) for t in range(n_tok)], axis=0)
        qbd = jnp.where(own_head, q_rows, 0.0).astype(BF16)
        qbd_scr[...] = qbd
        qf = qbd.astype(F32)
        tok_of_row = lax.shift_right_logical(lax.broadcasted_iota(jnp.int32, (n_rows, 1), 0),
                                             SB_HEADS.bit_length() - 1)
        nrun = jnp.zeros((n_rows, 1), F32)
        acc = jnp.zeros((n_rows, width), F32)
        for j in reversed(range(n_tok)):
            kj = kn_ref[j:j + 1, :].astype(BF16).astype(F32)
            vj = vn_ref[j:j + 1, :].astype(BF16).astype(F32)
            z = jnp.sum(qf * kj, axis=-1, keepdims=True) + bias
            w, nl = _sb_logs(jnp.where(tok_of_row > j, z, NEG_BIG))
            acc = acc + jnp.exp(w - nrun) * vj
            nrun = nrun + nl
        acc_scr[...] = acc
        run_scr[...] = nrun

    suffix = _suffix_matrix(k_refs[0].shape[1])
    qbd = qbd_scr[...]
    logs = [_sb_logs(_dot(qbd, k_ref[...].astype(BF16)) + bias) for k_ref in k_refs]
    sums = [_sb_suffix(nl, suffix) for _, nl in logs]
    nrun = run_scr[...]
    acc = acc_scr[...]
    for (w, _), (later, tot), v_ref in zip(logs, sums, v_refs):
        a = jnp.exp(w - later - nrun).astype(BF16)
        acc = acc + _dot_nt(a, v_ref[...].astype(BF16))
        nrun = nrun + tot
    acc_scr[...] = acc
    run_scr[...] = nrun

    @pl.when(step == pl.num_programs(1) - 1)
    def _():
        picked = jnp.where(own_head, acc, 0.0)
        for t in range(n_tok):
            o_ref[t:t + 1, :] = jnp.sum(picked[t * SB_HEADS:(t + 1) * SB_HEADS, :], axis=0, keepdims=True)


def _sb_sample(proj, cache_k, cache_v, page_table, sb_bias, *, n_seq, n_tok, pages_per_step):
    width = SB_HEADS * SB_HEAD_DIM
    n_pages = page_table.shape[1]
    page_size = cache_k.shape[1]
    steps = n_pages // pages_per_step
    p3 = proj.reshape(n_seq, n_tok, PROJ_W)
    ck = cache_k.transpose(0, 2, 3, 1).reshape(cache_k.shape[0], width, page_size)
    cv = cache_v.transpose(0, 2, 3, 1).reshape(cache_v.shape[0], width, page_size)
    bias_rows = jnp.tile(sb_bias, n_tok).reshape(n_tok * SB_HEADS, 1)
    blk = lambda col: col // width

    def new_rows(col):
        return pl.BlockSpec((None, n_tok, width), lambda n, s, pt: (n, 0, blk(col)))

    def page(u):
        return pl.BlockSpec((None, width, page_size),
                            lambda n, s, pt: (pt[n, n_pages - 1 - (s * pages_per_step + u)], 0, 0))

    n_rows = n_tok * SB_HEADS
    grid_spec = pltpu.PrefetchScalarGridSpec(
        num_scalar_prefetch=1,
        grid=(n_seq, steps),
        in_specs=[pl.BlockSpec((n_rows, 1), lambda n, s, pt: (0, 0)),
                  new_rows(COL_QB), new_rows(COL_KB), new_rows(COL_VB)]
                 + [page(u) for u in range(pages_per_step)] + [page(u) for u in range(pages_per_step)],
        out_specs=pl.BlockSpec((None, n_tok, width), lambda n, s, pt: (n, 0, 0)),
        scratch_shapes=[pltpu.VMEM((n_rows, width), BF16), pltpu.VMEM((n_rows, width), F32),
                        pltpu.VMEM((n_rows, 1), F32)],
    )
    out = pl.pallas_call(
        functools.partial(_sb_sample_kernel, n_tok=n_tok, pages_per_step=pages_per_step),
        grid_spec=grid_spec,
        out_shape=jax.ShapeDtypeStruct((n_seq, n_tok, width), F32),
        compiler_params=pltpu.CompilerParams(
            dimension_semantics=("arbitrary", "arbitrary"), vmem_limit_bytes=VMEM_LIMIT),
    )(page_table, bias_rows, p3, p3, p3, *([ck] * pages_per_step), *([cv] * pages_per_step))
    return out.reshape(n_seq * n_tok, width)


def _merge_kernel(oa_ref, ob_ref, ga_ref, gb_ref, h_ref, woa_ref, wob_ref, wout_ref, g_ref, out_ref):
    pa = _dot(oa_ref[...].astype(BF16), woa_ref[...])
    pb = _dot(ob_ref[...].astype(BF16), wob_ref[...])
    mixed = jax.nn.sigmoid(ga_ref[...]) * pa + jax.nn.sigmoid(gb_ref[...]) * pb
    out_ref[...] = h_ref[...] + _rms(_dot(mixed.astype(BF16), wout_ref[...]), g_ref[...])


def _merge(o_a, o_b, proj, h, w_o_gla, w_o_sb, w_out, g_post, *, tm):
    m, d = h.shape
    row = pl.BlockSpec((tm, d), lambda i: (i, 0))
    return pl.pallas_call(
        _merge_kernel,
        grid=(m // tm,),
        in_specs=[row, row,
                  pl.BlockSpec((tm, d), lambda i: (i, COL_GA // d)),
                  pl.BlockSpec((tm, d), lambda i: (i, COL_GB // d)),
                  row, _const_spec(w_o_gla.shape), _const_spec(w_o_sb.shape), _const_spec(w_out.shape),
                  _const_spec((1, d))],
        out_specs=row,
        out_shape=jax.ShapeDtypeStruct((m, d), F32),
        compiler_params=pltpu.CompilerParams(
            dimension_semantics=("arbitrary",), vmem_limit_bytes=VMEM_LIMIT),
    )(o_a, o_b, proj, proj, h, w_o_gla, w_o_sb, w_out, g_post)


def _row_tile(m, cap):
    return m if m <= cap else cap


def kernel(x_prompt, x_sample, cache_k, cache_v, state_gla, page_table, meta_tokens,
           ffn1_pre_g, ffn1_w_gate, ffn1_w_up, ffn1_w_down, ffn1_post_g,
           mix_pre_g, w_in, w_gk2, b_gk, gla_norm_g, sb_bias, w_o_gla, w_o_sb, w_out, mix_post_g,
           ffn2_pre_g, ffn2_w_gate, ffn2_w_up, ffn2_w_down, ffn2_post_g):
    n_seq, seq_len, d = x_prompt.shape
    n_dec, n_tok, _ = x_sample.shape
    assert ffn1_pre_g.shape[0] == 1, "single layer"
    qk = GLA_HEADS * GLA_DK
    v_w = GLA_HEADS * GLA_DV
    rank = w_gk2.shape[1]

    bf = lambda w: w[0].astype(BF16)
    w1g, w1u, w1d = bf(ffn1_w_gate), bf(ffn1_w_up), bf(ffn1_w_down)
    w2g, w2u, w2d = bf(ffn2_w_gate), bf(ffn2_w_up), bf(ffn2_w_down)
    w_in0 = w_in[0]
    gk0 = 2 * qk + v_w
    w_main = jnp.concatenate([w_in0[:, :gk0], w_in0[:, gk0 + rank:]], axis=1).astype(BF16)
    w_gk1 = jnp.pad(w_in0[:, gk0:gk0 + rank], ((0, 0), (0, LANES - rank))).astype(BF16)
    w_gk2p = jnp.pad(w_gk2[0], ((0, LANES - rank), (0, 0))).astype(BF16)
    woa, wob, wout = bf(w_o_gla), bf(w_o_sb), bf(w_out)
    row = lambda g: g[0].reshape(1, -1)
    bias = sb_bias[0]

    def pre_mix(x, tm):
        h, u = _ffn(x, row(ffn1_pre_g), w1g, w1u, w1d, row(ffn1_post_g), row(mix_pre_g), tm=tm)
        proj, glog = _proj(u, w_main, w_gk1, w_gk2p, row(b_gk), tm=_row_tile(x.shape[0], 1024), tn=1024)
        return h, u, proj, glog

    def post_mix(h, proj, o_a, o_b, tm):
        h2 = _merge(o_a, o_b, proj, h, woa, wob, wout, row(mix_post_g), tm=tm)
        return _ffn(h2, row(ffn2_pre_g), w2g, w2u, w2d, row(ffn2_post_g), tm=tm)

    x_meta = jnp.pad(meta_tokens.astype(F32), ((0, LANES - N_META), (0, 0)))
    _, u_m, proj_m, glog_m = pre_mix(x_meta, LANES)

    m_p = n_seq * seq_len
    tm_p = _row_tile(m_p, 512)
    h_p, u_p, proj_p, glog_p = pre_mix(x_prompt.reshape(m_p, d), tm_p)
    oa_p, s_p = _gla(proj_p, glog_p, row(gla_norm_g), n_seq=n_seq, seq_len=seq_len,
                     rows_per_step=_row_tile(seq_len, 512), chunk=GLA_CHUNK, n_heads=1,
                     meta=(proj_m, glog_m))
    ob_p = _sb_prompt(proj_p, proj_m, bias, n_seq=n_seq, seq_len=seq_len, tile=MXU_DIM)
    y_p = post_mix(h_p, proj_p, oa_p, ob_p, tm_p)

    m_s = n_dec * n_tok
    tm_s = _row_tile(m_s, 512)
    h_s, _, proj_s, glog_s = pre_mix(x_sample.reshape(m_s, d), tm_s)
    pad_tok = lambda a: jnp.pad(a.reshape(n_dec, n_tok, -1), ((0, 0), (0, SUBLANES - n_tok), (0, 0))
                                ).reshape(n_dec * SUBLANES, -1)
    oa_s, s_s = _gla(pad_tok(proj_s[:, :COL_QB]), pad_tok(glog_s), row(gla_norm_g), n_seq=n_dec,
                     seq_len=SUBLANES, rows_per_step=SUBLANES, chunk=SUBLANES, n_heads=GLA_HEADS,
                     state0=state_gla[0])
    oa_s = oa_s.reshape(n_dec, SUBLANES, v_w)[:, :n_tok].reshape(m_s, v_w)
    ob_s = _sb_sample(proj_s, cache_k[0], cache_v[0], page_table, bias,
                      n_seq=n_dec, n_tok=n_tok, pages_per_step=4)
    y_s = post_mix(h_s, proj_s, oa_s, ob_s, tm_s)

    heads = (SB_HEADS, SB_HEAD_DIM)
    w_sb = SB_HEADS * SB_HEAD_DIM

    u_full = jnp.concatenate([jnp.broadcast_to(u_m[:N_META][None], (n_seq, N_META, d)),
                              u_p.reshape(n_seq, seq_len, d)], axis=1)
    k_t, v_t = _kv_t(w_main[:, COL_KB:COL_KB + w_sb].T, w_main[:, COL_VB:COL_VB + w_sb].T, u_full,
                     tp=512)
    kv_out = lambda a: a.reshape(n_seq, *heads, N_META + seq_len).transpose(0, 3, 1, 2)[None]

    return (y_p.reshape(n_seq, seq_len, d),
            y_s.reshape(n_dec, n_tok, d),
            kv_out(k_t), kv_out(v_t),
            s_p[None],
            proj_s[:, COL_KB:COL_KB + w_sb].reshape(1, n_dec, n_tok, *heads),
            proj_s[:, COL_VB:COL_VB + w_sb].reshape(1, n_dec, n_tok, *heads),
            s_s[None])
```

```python
import functools

import jax
import jax.numpy as jnp
from jax import lax
from jax.experimental import pallas as pl
from jax.experimental.pallas import tpu as pltpu

F32 = jnp.float32
BF16 = jnp.bfloat16

EPS = 1e-6
N_META = 16
GLA_HEADS = 4
GLA_DK = 128
GLA_DV = 256
GLA_TAU = 16.0
GLA_CHUNK = 64
SB_HEADS = 16
SB_HEAD_DIM = 64
LANES = 128
SUBLANES = 8
MXU_DIM = 256
HEADS_PER_SLAB = LANES // SB_HEAD_DIM
VMEM_LIMIT = 48 * 1024 * 1024
NEG_BIG = -1e30
LOG2E = 1.4426950408889634
SB_VISITS_PER_TRIP = 4

COL_QA, COL_KA, COL_VA, COL_RA = 0, 512, 1024, 2048
COL_QB, COL_KB, COL_VB, COL_GA, COL_GB = 3072, 4096, 5120, 6144, 7168
PROJ_W = 8192


def _rms(x, g):
    return x * lax.rsqrt(jnp.mean(x * x, axis=-1, keepdims=True) + EPS) * g


def _const_spec(shape):
    return pl.BlockSpec(shape, lambda *_: (0,) * len(shape), pipeline_mode=pl.Buffered(1))


def _dot(a, b):
    return jnp.dot(a, b, preferred_element_type=F32)


def _dot_nt(a, b):
    return lax.dot_general(a, b, (((1,), (1,)), ((), ())), preferred_element_type=F32)


def _dot_tn(a, b):
    return lax.dot_general(a, b, (((0,), (0,)), ((), ())), preferred_element_type=F32)


def _split3(x):
    hi = x.astype(BF16)
    r1 = x - hi.astype(F32)
    mid = r1.astype(BF16)
    lo = (r1 - mid.astype(F32)).astype(BF16)
    return hi, mid, lo


def _split2(x):
    hi = x.astype(BF16)
    lo = (x - hi.astype(F32)).astype(BF16)
    return hi, lo


def _ffn_kernel(x_ref, gpre_ref, wg_ref, wu_ref, wd_ref, gpost_ref, *rest, ff_chunk, emit_u):
    if emit_u:
        gnext_ref, h_ref, u_ref, act_ref = rest
    else:
        h_ref, act_ref = rest
    x = x_ref[...]
    u = _rms(x, gpre_ref[...]).astype(BF16)
    d_ff = wg_ref.shape[1]
    for c in range(d_ff // ff_chunk):
        sl = slice(c * ff_chunk, (c + 1) * ff_chunk)
        gate = _dot(u, wg_ref[:, sl])
        up = _dot(u, wu_ref[:, sl])
        act_ref[:, sl] = (gate * jax.nn.sigmoid(gate) * up).astype(BF16)
    y = _dot(act_ref[...], wd_ref[...])
    h = x + 0.5 * _rms(y, gpost_ref[...])
    h_ref[...] = h
    if emit_u:
        u_ref[...] = _rms(h, gnext_ref[...]).astype(BF16)


def _ffn(x, g_pre, wg, wu, wd, g_post, g_next=None, *, tm):
    m, d = x.shape
    d_ff = wg.shape[1]
    emit_u = g_next is not None
    row = pl.BlockSpec((tm, d), lambda i: (i, 0))
    in_specs = [row, _const_spec((1, d)), _const_spec((d, d_ff)), _const_spec((d, d_ff)),
                _const_spec((d_ff, d)), _const_spec((1, d))]
    args = [x, g_pre, wg, wu, wd, g_post]
    out_shape = [jax.ShapeDtypeStruct((m, d), F32)]
    out_specs = [row]
    if emit_u:
        in_specs.append(_const_spec((1, d)))
        args.append(g_next)
        out_shape.append(jax.ShapeDtypeStruct((m, d), BF16))
        out_specs.append(row)
    outs = pl.pallas_call(
        functools.partial(_ffn_kernel, ff_chunk=MXU_DIM, emit_u=emit_u),
        grid=(m // tm,),
        in_specs=in_specs,
        out_specs=out_specs,
        out_shape=out_shape,
        scratch_shapes=[pltpu.VMEM((tm, d_ff), BF16)],
        compiler_params=pltpu.CompilerParams(
            dimension_semantics=("arbitrary",), vmem_limit_bytes=VMEM_LIMIT),
    )(*args)
    return outs if emit_u else outs[0]


def _proj_kernel(u_ref, w_ref, wgk1_ref, wgk2_ref, bgk_ref, p_ref, g_ref):
    u = u_ref[...]
    p_ref[...] = _dot(u, w_ref[...])

    @pl.when(pl.program_id(1) == 0)
    def _():
        low = _dot(u, wgk1_ref[...]).astype(BF16)
        zg = _dot(low, wgk2_ref[...]) + bgk_ref[...]
        g_ref[...] = (jnp.minimum(zg, 0.0) - jnp.log(1.0 + jnp.exp(-jnp.abs(zg)))) * (1.0 / GLA_TAU)


def _proj(u, w_main, w_gk1, w_gk2, b_gk, *, tm, tn):
    m, d = u.shape
    n = w_main.shape[1]
    qk = w_gk2.shape[1]
    return pl.pallas_call(
        _proj_kernel,
        grid=(m // tm, n // tn),
        in_specs=[pl.BlockSpec((tm, d), lambda i, j: (i, 0)),
                  pl.BlockSpec((d, tn), lambda i, j: (0, j)),
                  _const_spec(w_gk1.shape), _const_spec(w_gk2.shape), _const_spec((1, qk))],
        out_specs=[pl.BlockSpec((tm, tn), lambda i, j: (i, j)),
                   pl.BlockSpec((tm, qk), lambda i, j: (i, 0))],
        out_shape=[jax.ShapeDtypeStruct((m, n), F32), jax.ShapeDtypeStruct((m, qk), F32)],
        compiler_params=pltpu.CompilerParams(
            dimension_semantics=("arbitrary", "arbitrary"), vmem_limit_bytes=VMEM_LIMIT),
    )(u, w_main, w_gk1, w_gk2, b_gk)


def _kv_t_kernel(wk_ref, wv_ref, u_ref, k_ref, v_ref):
    u = u_ref[...]
    k_ref[...] = _dot_nt(wk_ref[...], u)
    v_ref[...] = _dot_nt(wv_ref[...], u)


def _kv_t(wk_t, wv_t, u_full, *, tp):
    width, d = wk_t.shape
    n_seq, n_pos, _ = u_full.shape
    out = pl.BlockSpec((None, width, tp), lambda b, j: (b, 0, j))
    return pl.pallas_call(
        _kv_t_kernel,
        grid=(n_seq, pl.cdiv(n_pos, tp)),
        in_specs=[_const_spec((width, d)), _const_spec((width, d)),
                  pl.BlockSpec((None, tp, d), lambda b, j: (b, j, 0))],
        out_specs=[out, out],
        out_shape=[jax.ShapeDtypeStruct((n_seq, width, n_pos), F32)] * 2,
        compiler_params=pltpu.CompilerParams(
            dimension_semantics=("arbitrary", "arbitrary"), vmem_limit_bytes=VMEM_LIMIT),
    )(wk_t, wv_t, u_full)


def _gla_local(problems, c):
    rows = lax.broadcasted_iota(jnp.int32, (c, c), 0)
    cols = lax.broadcasted_iota(jnp.int32, (c, c), 1)
    causal = cols <= rows
    tri = causal.astype(BF16)
    ones_c = jnp.ones((c, LANES), BF16)
    parts = [_split3(g) for _, _, _, g in problems]
    cums = [_dot(tri, p[0]) + _dot(tri, p[1]) + _dot(tri, p[2]) for p in parts]
    sums = [_dot_tn(p[0], ones_c) + _dot_tn(p[1], ones_c) + _dot_tn(p[2], ones_c) for p in parts]
    q_ins, k_ins, k_ends, vbs = [], [], [], []
    for (q, k, v, _), b in zip(problems, cums):
        q_ins.append((q * (GLA_DK ** -0.5) * jnp.exp(b)).astype(BF16))
        k_ins.append((k * jnp.exp(-b)).astype(BF16))
        k_ends.append((k * jnp.exp(b[c - 1:c, :] - b)).astype(BF16))
        vbs.append(v.astype(BF16))
    scores = [_dot_nt(qi, ki) for qi, ki in zip(q_ins, k_ins)]
    o_intra = [_dot(jnp.where(causal, a, 0.0).astype(BF16), vb) for a, vb in zip(scores, vbs)]
    d_state = [_dot_tn(ke, vb) for ke, vb in zip(k_ends, vbs)]
    decays = [jnp.concatenate([jnp.exp(bs)] * (GLA_DV // LANES), axis=1) for bs in sums]
    return list(zip(q_ins, o_intra, d_state, decays))


def _gla_kernel(*refs, n_heads, chunk, n_chunks, lead):
    if lead:
        km_ref, vm_ref, gm_ref = refs[:3]
        refs = refs[3:]
    else:
        s0_ref = refs[0]
        refs = refs[1:]
    q_ref, k_ref, v_ref, g_ref, r_ref, gn_ref, o_ref, sfin_ref, s_scr = refs
    step = pl.program_id(2)

    @pl.when(step == 0)
    def _():
        if lead:
            zeros = jnp.zeros((lead, GLA_DK), F32)
            (_, _, d_state, _), = _gla_local([(zeros, km_ref[...], vm_ref[...], gm_ref[...])], lead)
            s_scr[0] = d_state
        else:
            s_scr[...] = s0_ref[...]

    dk = lambda h: slice(h * GLA_DK, (h + 1) * GLA_DK)
    dv = lambda h: slice(h * GLA_DV, (h + 1) * GLA_DV)
    problems = []
    for h in range(n_heads):
        for ci in range(n_chunks):
            sl = slice(ci * chunk, (ci + 1) * chunk)
            problems.append((q_ref[sl, dk(h)], k_ref[sl, dk(h)], v_ref[sl, dv(h)], g_ref[sl, dk(h)]))
    local = _gla_local(problems, chunk)
    gn = gn_ref[...]
    for h in range(n_heads):
        s = s_scr[h]
        for ci in range(n_chunks):
            sl = slice(ci * chunk, (ci + 1) * chunk)
            q_in, o_intra, d_state, decay = local[h * n_chunks + ci]
            o = o_intra + _dot(q_in, s.astype(BF16))
            s = decay * s + d_state
            r = r_ref[sl, dv(h)]
            o_ref[sl, dv(h)] = (_rms(o, gn) * (r * jax.nn.sigmoid(r))).astype(BF16)
        s_scr[h] = s

    @pl.when(step == pl.num_programs(2) - 1)
    def _():
        sfin_ref[...] = s_scr[...]


def _gla(proj, glog, gla_norm_g, *, n_seq, seq_len, rows_per_step, chunk, n_heads, meta=None, state0=None):
    steps = seq_len // rows_per_step
    n_chunks = rows_per_step // chunk
    groups = GLA_HEADS // n_heads
    wk, wv = n_heads * GLA_DK, n_heads * GLA_DV

    def rows(width, col0):
        return pl.BlockSpec((rows_per_step, width), lambda b, h, s: (b * steps + s, col0 // width + h))

    state_spec = pl.BlockSpec((None, n_heads, GLA_DK, GLA_DV), lambda b, h, s: (b, h, 0, 0))
    in_specs, args = [], []
    if meta is not None:
        assert n_heads == 1
        pm, gm = meta
        in_specs += [pl.BlockSpec((N_META, GLA_DK), lambda b, h, s: (0, COL_KA // GLA_DK + h)),
                     pl.BlockSpec((N_META, GLA_DV), lambda b, h, s: (0, COL_VA // GLA_DV + h)),
                     pl.BlockSpec((N_META, GLA_DK), lambda b, h, s: (0, h))]
        args += [pm, pm, gm]
    else:
        in_specs.append(state_spec)
        args.append(state0)
    in_specs += [rows(wk, COL_QA), rows(wk, COL_KA), rows(wv, COL_VA), rows(wk, 0), rows(wv, COL_RA),
                 _const_spec((1, GLA_DV))]
    args += [proj, proj, proj, glog, proj, gla_norm_g]
    m = n_seq * seq_len
    return pl.pallas_call(
        functools.partial(_gla_kernel, n_heads=n_heads, chunk=chunk, n_chunks=n_chunks,
                          lead=N_META if meta is not None else 0),
        grid=(n_seq, groups, steps),
        in_specs=in_specs,
        out_specs=[rows(wv, 0), state_spec],
        out_shape=[jax.ShapeDtypeStruct((m, GLA_HEADS * GLA_DV), BF16),
                   jax.ShapeDtypeStruct((n_seq, GLA_HEADS, GLA_DK, GLA_DV), F32)],
        scratch_shapes=[pltpu.VMEM((n_heads, GLA_DK, GLA_DV), F32)],
        compiler_params=pltpu.CompilerParams(
            dimension_semantics=("arbitrary", "arbitrary", "arbitrary"), vmem_limit_bytes=VMEM_LIMIT),
    )(*args)


def _sb_logs(zb):
    nl = jnp.maximum(zb, 0.0) + jnp.log(1.0 + jnp.exp2(jnp.abs(zb) * (-LOG2E)))
    return zb - nl, nl


def _suffix_matrix(t):
    rows = lax.broadcasted_iota(jnp.int32, (t, t), 0)
    cols = lax.broadcasted_iota(jnp.int32, (t, t), 1)
    upper = (rows > cols).astype(BF16)
    return jnp.concatenate([upper, upper], axis=0)


def _sb_suffix(nl, suffix):
    hi, lo = _split2(nl)
    later = _dot(jnp.concatenate([hi, lo], axis=1), suffix)
    return later, later[:, 0:1] + nl[:, 0:1]


def _sb_prompt_kernel(vq_ref, vt_ref, bias_ref, q_ref, k_ref, v_ref, km_ref, vm_ref, o_ref,
                      q_scr, k_scr, vv_scr, bt_scr, sfx_scr, z_scr, w_scr, later_scr, tot_scr, nrun_scr, acc_scr,
                      *, tile, n_main, n_visits):
    pair = pl.program_id(1)
    low_lanes = lax.broadcasted_iota(jnp.int32, (1, LANES), 1) < SB_HEAD_DIM
    both = lambda x: jnp.concatenate([jnp.where(low_lanes, x, 0.0), jnp.where(low_lanes, 0.0, x)], axis=0)
    DIAG, FULL, META, NONE = range(4)

    for j in range(n_main):
        sl = slice(j * tile, (j + 1) * tile)
        q_scr[j] = both(q_ref[sl, :] * (SB_HEAD_DIM ** -0.5)).astype(BF16)
        k_scr[j] = k_ref[sl, :].astype(BF16)
        vv_scr[j] = both(v_ref[sl, :]).astype(BF16)
    pad = jnp.zeros((tile - LANES, LANES), F32)
    k_scr[n_main] = jnp.concatenate([km_ref[...], pad], axis=0).astype(BF16)
    vv_scr[n_main] = both(jnp.concatenate([vm_ref[...], pad], axis=0)).astype(BF16)
    rows = lax.broadcasted_iota(jnp.int32, (tile, tile), 0)
    cols = lax.broadcasted_iota(jnp.int32, (tile, tile), 1)
    kinds = {DIAG: jnp.where(cols < rows, 0.0, NEG_BIG), FULL: jnp.zeros((tile, tile), F32),
             META: jnp.where(cols < N_META, 0.0, NEG_BIG), NONE: jnp.full((tile, tile), NEG_BIG, F32)}
    for kind, mask in kinds.items():
        for h in range(HEADS_PER_SLAB):
            bt_scr[kind, h * tile:(h + 1) * tile, :] = mask + bias_ref[HEADS_PER_SLAB * pair + h]
    sfx_scr[...] = _suffix_matrix(tile)

    def key_slot(q, t):
        return jnp.where(t <= q, q - t, n_main)

    def kind_of(q, t):
        return jnp.where(t == 0, DIAG, jnp.where(t <= q, FULL, jnp.where(t == q + 1, META, NONE)))

    def logits(v):
        q, t = vq_ref[v], vt_ref[v]
        return _dot_nt(q_scr[q], k_scr[key_slot(q, t)])

    def stage(v, cur, nxt):
        z_scr[nxt] = logits(v + 1)
        w, nl = _sb_logs(z_scr[cur] + bt_scr[kind_of(vq_ref[v], vt_ref[v])])
        later, tot = _sb_suffix(nl, sfx_scr[...])
        prev = jnp.maximum(v - 1, 0)
        q, t = vq_ref[prev], vt_ref[prev]
        fresh = t == 0
        nrun = jnp.where(fresh, 0.0, nrun_scr[...])
        a = jnp.exp(w_scr[nxt] - later_scr[nxt] - jnp.concatenate([nrun, nrun], axis=1)).astype(BF16)
        acc = jnp.where(fresh, 0.0, acc_scr[...]) + _dot(jnp.concatenate([a[:tile], a[tile:]], axis=1),
                                                          vv_scr[key_slot(q, t)])
        acc_scr[...] = acc
        o_ref[pl.ds(pl.multiple_of(q * tile, tile), tile), :] = acc.astype(BF16)
        nrun_scr[...] = nrun + tot_scr[nxt]
        w_scr[cur] = w
        later_scr[cur] = later
        tot_scr[cur] = jnp.broadcast_to(tot, tot_scr.shape[1:])

    z_scr[0] = logits(0)
    w_scr[1] = jnp.full(w_scr.shape[1:], NEG_BIG, F32)
    later_scr[1] = jnp.zeros(later_scr.shape[1:], F32)
    tot_scr[1] = jnp.zeros(tot_scr.shape[1:], F32)
    nrun_scr[...] = jnp.zeros(nrun_scr.shape, F32)
    acc_scr[...] = jnp.zeros(acc_scr.shape, F32)

    def body(u, carry):
        for i in range(SB_VISITS_PER_TRIP):
            stage(SB_VISITS_PER_TRIP * u + i, i % 2, (i + 1) % 2)
        return carry

    lax.fori_loop(0, pl.cdiv(n_visits + 1, SB_VISITS_PER_TRIP), body, 0)


def _sb_prompt(proj, proj_meta, sb_bias, *, n_seq, seq_len, tile):
    p3 = proj.reshape(n_seq, seq_len, PROJ_W)
    slab = lambda col: col // LANES
    n_main = seq_len // tile
    pair_rows = HEADS_PER_SLAB * tile
    visits = [(q, t) for q in range(n_main) for t in range(q + 2)]
    n_visits = len(visits)
    visits += [(n_main - 1, n_main + 1)] * (SB_VISITS_PER_TRIP + 2)
    vq = jnp.asarray([q for q, _ in visits], jnp.int32)
    vt = jnp.asarray([t for _, t in visits], jnp.int32)
    seq_blk = lambda col: pl.BlockSpec((None, seq_len, LANES), lambda b, p, *_: (b, 0, slab(col) + p))
    meta_blk = lambda col: pl.BlockSpec((LANES, LANES), lambda b, p, *_: (0, slab(col) + p))
    grid_spec = pltpu.PrefetchScalarGridSpec(
        num_scalar_prefetch=2,
        grid=(n_seq, SB_HEADS // HEADS_PER_SLAB),
        in_specs=[pl.BlockSpec(memory_space=pltpu.SMEM),
                  seq_blk(COL_QB), seq_blk(COL_KB), seq_blk(COL_VB), meta_blk(COL_KB), meta_blk(COL_VB)],
        out_specs=pl.BlockSpec((None, seq_len, LANES), lambda b, p, *_: (b, 0, p)),
        scratch_shapes=[pltpu.VMEM((n_main, pair_rows, LANES), BF16),
                        pltpu.VMEM((n_main + 1, tile, LANES), BF16),
                        pltpu.VMEM((n_main + 1, pair_rows, LANES), BF16),
                        pltpu.VMEM((4, pair_rows, tile), F32),
                        pltpu.VMEM((2 * tile, tile), BF16),
                        pltpu.VMEM((2, pair_rows, tile), F32),
                        pltpu.VMEM((2, pair_rows, tile), F32),
                        pltpu.VMEM((2, pair_rows, tile), F32),
                        pltpu.VMEM((2, pair_rows, LANES), F32),
                        pltpu.VMEM((pair_rows, LANES), F32),
                        pltpu.VMEM((tile, LANES), F32)],
    )
    return pl.pallas_call(
        functools.partial(_sb_prompt_kernel, tile=tile, n_main=n_main, n_visits=n_visits),
        grid_spec=grid_spec,
        out_shape=jax.ShapeDtypeStruct((n_seq, seq_len, SB_HEADS * SB_HEAD_DIM), BF16),
        compiler_params=pltpu.CompilerParams(
            dimension_semantics=("arbitrary", "arbitrary"), vmem_limit_bytes=VMEM_LIMIT),
    )(vq, vt, sb_bias, p3, p3, p3, proj_meta, proj_meta).reshape(n_seq * seq_len, SB_HEADS * SB_HEAD_DIM)


def _sb_sample_kernel(pt_ref, bias_ref, q_ref, kn_ref, vn_ref, *rest, n_tok, pages_per_step):
    k_refs = rest[:pages_per_step]
    v_refs = rest[pages_per_step:2 * pages_per_step]
    o_ref, qbd_scr, acc_scr, run_scr = rest[2 * pages_per_step:]
    step = pl.program_id(1)
    n_rows = n_tok * SB_HEADS
    width = SB_HEADS * SB_HEAD_DIM
    row_id = lax.broadcasted_iota(jnp.int32, (n_rows, width), 0)
    col_id = lax.broadcasted_iota(jnp.int32, (n_rows, width), 1)
    own_head = lax.shift_right_logical(col_id, SB_HEAD_DIM.bit_length() - 1) == (row_id & (SB_HEADS - 1))
    bias = bias_ref[...]

    @pl.when(step == 0)
    def _():
        q = q_ref[...] * (SB_HEAD_DIM ** -0.5)
        q_rows = jnp.concatenate(
            [jnp.broadcast_to(q[t:t + 1, :], (SB_HEADS, width)) for t in range(n_tok)], axis=0)
        qbd = jnp.where(own_head, q_rows, 0.0).astype(BF16)
        qbd_scr[...] = qbd
        qf = qbd.astype(F32)
        tok_of_row = lax.shift_right_logical(lax.broadcasted_iota(jnp.int32, (n_rows, 1), 0),
                                             SB_HEADS.bit_length() - 1)
        nrun = jnp.zeros((n_rows, 1), F32)
        acc = jnp.zeros((n_rows, width), F32)
        for j in reversed(range(n_tok)):
            kj = kn_ref[j:j + 1, :].astype(BF16).astype(F32)
            vj = vn_ref[j:j + 1, :].astype(BF16).astype(F32)
            z = jnp.sum(qf * kj, axis=-1, keepdims=True) + bias
            w, nl = _sb_logs(jnp.where(tok_of_row > j, z, NEG_BIG))
            acc = acc + jnp.exp(w - nrun) * vj
            nrun = nrun + nl
        acc_scr[...] = acc
        run_scr[...] = nrun

    suffix = _suffix_matrix(k_refs[0].shape[1])
    qbd = qbd_scr[...]
    logs = [_sb_logs(_dot(qbd, k_ref[...].astype(BF16)) + bias) for k_ref in k_refs]
    sums = [_sb_suffix(nl, suffix) for _, nl in logs]
    nrun = run_scr[...]
    acc = acc_scr[...]
    for (w, _), (later, tot), v_ref in zip(logs, sums, v_refs):
        a = jnp.exp(w - later - nrun).astype(BF16)
        acc = acc + _dot_nt(a, v_ref[...].astype(BF16))
        nrun = nrun + tot
    acc_scr[...] = acc
    run_scr[...] = nrun

    @pl.when(step == pl.num_programs(1) - 1)
    def _():
        picked = jnp.where(own_head, acc, 0.0)
        for t in range(n_tok):
            o_ref[t:t + 1, :] = jnp.sum(picked[t * SB_HEADS:(t + 1) * SB_HEADS, :], axis=0, keepdims=True)


def _sb_sample(proj, cache_k, cache_v, page_table, sb_bias, *, n_seq, n_tok, pages_per_step):
    width = SB_HEADS * SB_HEAD_DIM
    n_pages = page_table.shape[1]
    page_size = cache_k.shape[1]
    steps = n_pages // pages_per_step
    p3 = proj.reshape(n_seq, n_tok, PROJ_W)
    ck = cache_k.transpose(0, 2, 3, 1).reshape(cache_k.shape[0], width, page_size)
    cv = cache_v.transpose(0, 2, 3, 1).reshape(cache_v.shape[0], width, page_size)
    bias_rows = jnp.tile(sb_bias, n_tok).reshape(n_tok * SB_HEADS, 1)
    blk = lambda col: col // width

    def new_rows(col):
        return pl.BlockSpec((None, n_tok, width), lambda n, s, pt: (n, 0, blk(col)))

    def page(u):
        return pl.BlockSpec((None, width, page_size),
                            lambda n, s, pt: (pt[n, n_pages - 1 - (s * pages_per_step + u)], 0, 0))

    n_rows = n_tok * SB_HEADS
    grid_spec = pltpu.PrefetchScalarGridSpec(
        num_scalar_prefetch=1,
        grid=(n_seq, steps),
        in_specs=[pl.BlockSpec((n_rows, 1), lambda n, s, pt: (0, 0)),
                  new_rows(COL_QB), new_rows(COL_KB), new_rows(COL_VB)]
                 + [page(u) for u in range(pages_per_step)] + [page(u) for u in range(pages_per_step)],
        out_specs=pl.BlockSpec((None, n_tok, width), lambda n, s, pt: (n, 0, 0)),
        scratch_shapes=[pltpu.VMEM((n_rows, width), BF16), pltpu.VMEM((n_rows, width), F32),
                        pltpu.VMEM((n_rows, 1), F32)],
    )
    out = pl.pallas_call(
        functools.partial(_sb_sample_kernel, n_tok=n_tok, pages_per_step=pages_per_step),
        grid_spec=grid_spec,
        out_shape=jax.ShapeDtypeStruct((n_seq, n_tok, width), F32),
        compiler_params=pltpu.CompilerParams(
            dimension_semantics=("arbitrary", "arbitrary"), vmem_limit_bytes=VMEM_LIMIT),
    )(page_table, bias_rows, p3, p3, p3, *([ck] * pages_per_step), *([cv] * pages_per_step))
    return out.reshape(n_seq * n_tok, width)


def _merge_kernel(oa_ref, ob_ref, ga_ref, gb_ref, h_ref, woa_ref, wob_ref, wout_ref, g_ref, out_ref):
    pa = _dot(oa_ref[...].astype(BF16), woa_ref[...])
    pb = _dot(ob_ref[...].astype(BF16), wob_ref[...])
    mixed = jax.nn.sigmoid(ga_ref[...]) * pa + jax.nn.sigmoid(gb_ref[...]) * pb
    out_ref[...] = h_ref[...] + _rms(_dot(mixed.astype(BF16), wout_ref[...]), g_ref[...])


def _merge(o_a, o_b, proj, h, w_o_gla, w_o_sb, w_out, g_post, *, tm):
    m, d = h.shape
    row = pl.BlockSpec((tm, d), lambda i: (i, 0))
    return pl.pallas_call(
        _merge_kernel,
        grid=(m // tm,),
        in_specs=[row, row,
                  pl.BlockSpec((tm, d), lambda i: (i, COL_GA // d)),
                  pl.BlockSpec((tm, d), lambda i: (i, COL_GB // d)),
                  row, _const_spec(w_o_gla.shape), _const_spec(w_o_sb.shape), _const_spec(w_out.shape),
                  _const_spec((1, d))],
        out_specs=row,
        out_shape=jax.ShapeDtypeStruct((m, d), F32),
        compiler_params=pltpu.CompilerParams(
            dimension_semantics=("arbitrary",), vmem_limit_bytes=VMEM_LIMIT),
    )(o_a, o_b, proj, proj, h, w_o_gla, w_o_sb, w_out, g_post)


def _row_tile(m, cap):
    return m if m <= cap else cap


def kernel(x_prompt, x_sample, cache_k, cache_v, state_gla, page_table, meta_tokens,
           ffn1_pre_g, ffn1_w_gate, ffn1_w_up, ffn1_w_down, ffn1_post_g,
           mix_pre_g, w_in, w_gk2, b_gk, gla_norm_g, sb_bias, w_o_gla, w_o_sb, w_out, mix_post_g,
           ffn2_pre_g, ffn2_w_gate, ffn2_w_up, ffn2_w_down, ffn2_post_g):
    n_seq, seq_len, d = x_prompt.shape
    n_dec, n_tok, _ = x_sample.shape
    assert ffn1_pre_g.shape[0] == 1, "single layer"
    qk = GLA_HEADS * GLA_DK
    v_w = GLA_HEADS * GLA_DV
    rank = w_gk2.shape[1]

    bf = lambda w: w[0].astype(BF16)
    w1g, w1u, w1d = bf(ffn1_w_gate), bf(ffn1_w_up), bf(ffn1_w_down)
    w2g, w2u, w2d = bf(ffn2_w_gate), bf(ffn2_w_up), bf(ffn2_w_down)
    w_in0 = w_in[0]
    gk0 = 2 * qk + v_w
    w_main = jnp.concatenate([w_in0[:, :gk0], w_in0[:, gk0 + rank:]], axis=1).astype(BF16)
    w_gk1 = jnp.pad(w_in0[:, gk0:gk0 + rank], ((0, 0), (0, LANES - rank))).astype(BF16)
    w_gk2p = jnp.pad(w_gk2[0], ((0, LANES - rank), (0, 0))).astype(BF16)
    woa, wob, wout = bf(w_o_gla), bf(w_o_sb), bf(w_out)
    row = lambda g: g[0].reshape(1, -1)
    bias = sb_bias[0]

    def pre_mix(x, tm):
        h, u = _ffn(x, row(ffn1_pre_g), w1g, w1u, w1d, row(ffn1_post_g), row(mix_pre_g), tm=tm)
        proj, glog = _proj(u, w_main, w_gk1, w_gk2p, row(b_gk), tm=_row_tile(x.shape[0], 1024), tn=1024)
        return h, u, proj, glog

    def post_mix(h, proj, o_a, o_b, tm):
        h2 = _merge(o_a, o_b, proj, h, woa, wob, wout, row(mix_post_g), tm=tm)
        return _ffn(h2, row(ffn2_pre_g), w2g, w2u, w2d, row(ffn2_post_g), tm=tm)

    x_meta = jnp.pad(meta_tokens.astype(F32), ((0, LANES - N_META), (0, 0)))
    _, u_m, proj_m, glog_m = pre_mix(x_meta, LANES)

    m_p = n_seq * seq_len
    tm_p = _row_tile(m_p, 512)
    h_p, u_p, proj_p, glog_p = pre_mix(x_prompt.reshape(m_p, d), tm_p)
    oa_p, s_p = _gla(proj_p, glog_p, row(gla_norm_g), n_seq=n_seq, seq_len=seq_len,
                     rows_per_step=_row_tile(seq_len, 512), chunk=GLA_CHUNK, n_heads=1,
                     meta=(proj_m, glog_m))
    ob_p = _sb_prompt(proj_p, proj_m, bias, n_seq=n_seq, seq_len=seq_len, tile=MXU_DIM)
    y_p = post_mix(h_p, proj_p, oa_p, ob_p, tm_p)

    m_s = n_dec * n_tok
    tm_s = _row_tile(m_s, 512)
    h_s, _, proj_s, glog_s = pre_mix(x_sample.reshape(m_s, d), tm_s)
    pad_tok = lambda a: jnp.pad(a.reshape(n_dec, n_tok, -1), ((0, 0), (0, SUBLANES - n_tok), (0, 0))
                                ).reshape(n_dec * SUBLANES, -1)
    oa_s, s_s = _gla(pad_tok(proj_s[:, :COL_QB]), pad_tok(glog_s), row(gla_norm_g), n_seq=n_dec,
                     seq_len=SUBLANES, rows_per_step=SUBLANES, chunk=SUBLANES, n_heads=GLA_HEADS,
                     state0=state_gla[0])
    oa_s = oa_s.reshape(n_dec, SUBLANES, v_w)[:, :n_tok].reshape(m_s, v_w)
    ob_s = _sb_sample(proj_s, cache_k[0], cache_v[0], page_table, bias,
                      n_seq=n_dec, n_tok=n_tok, pages_per_step=8)
    y_s = post_mix(h_s, proj_s, oa_s, ob_s, tm_s)

    heads = (SB_HEADS, SB_HEAD_DIM)
    w_sb = SB_HEADS * SB_HEAD_DIM

    u_full = jnp.concatenate([jnp.broadcast_to(u_m[:N_META][None], (n_seq, N_META, d)),
                              u_p.reshape(n_seq, seq_len, d)], axis=1)
    k_t, v_t = _kv_t(w_main[:, COL_KB:COL_KB + w_sb].T, w_main[:, COL_VB:COL_VB + w_sb].T, u_full,
                     tp=512)
    kv_out = lambda a: a.reshape(n_seq, *heads, N_META + seq_len).transpose(0, 3, 1, 2)[None]

    return (y_p.reshape(n_seq, seq_len, d),
            y_s.reshape(n_dec, n_tok, d),
            kv_out(k_t), kv_out(v_t),
            s_p[None],
            proj_s[:, COL_KB:COL_KB + w_sb].reshape(1, n_dec, n_tok, *heads),
            proj_s[:, COL_VB:COL_VB + w_sb].reshape(1, n_dec, n_tok, *heads),
            s_s[None])
```

```python
import functools

import jax
import jax.numpy as jnp
from jax import lax
from jax.experimental import pallas as pl
from jax.experimental.pallas import tpu as pltpu

F32 = jnp.float32
BF16 = jnp.bfloat16

EPS = 1e-6
N_META = 16
GLA_HEADS = 4
GLA_DK = 128
GLA_DV = 256
GLA_TAU = 16.0
GLA_CHUNK = 64
SB_HEADS = 16
SB_HEAD_DIM = 64
LANES = 128
SUBLANES = 8
MXU_DIM = 256
HEADS_PER_SLAB = LANES // SB_HEAD_DIM
VMEM_LIMIT = 56 * 1024 * 1024
NEG_BIG = -1e30
LOG2E = 1.4426950408889634
SB_VISITS_PER_TRIP = 4
SB_PAGES_PER_TRIP = 2
SB_PREFETCH_TRIPS = 4

COL_QA, COL_KA, COL_VA, COL_RA = 0, 512, 1024, 2048
COL_QB, COL_KB, COL_VB, COL_GA, COL_GB = 3072, 4096, 5120, 6144, 7168
PROJ_W = 8192


def _rms(x, g):
    return x * lax.rsqrt(jnp.mean(x * x, axis=-1, keepdims=True) + EPS) * g


def _const_spec(shape):
    return pl.BlockSpec(shape, lambda *_: (0,) * len(shape), pipeline_mode=pl.Buffered(1))


def _dot(a, b):
    return jnp.dot(a, b, preferred_element_type=F32)


def _dot_nt(a, b):
    return lax.dot_general(a, b, (((1,), (1,)), ((), ())), preferred_element_type=F32)


def _dot_tn(a, b):
    return lax.dot_general(a, b, (((0,), (0,)), ((), ())), preferred_element_type=F32)


def _split3(x):
    hi = x.astype(BF16)
    r1 = x - hi.astype(F32)
    mid = r1.astype(BF16)
    lo = (r1 - mid.astype(F32)).astype(BF16)
    return hi, mid, lo


def _split2(x):
    hi = x.astype(BF16)
    lo = (x - hi.astype(F32)).astype(BF16)
    return hi, lo


def _ffn_kernel(x_ref, gpre_ref, wg_ref, wu_ref, wd_ref, gpost_ref, *rest, ff_chunk, emit_u):
    if emit_u:
        gnext_ref, h_ref, u_ref, act_ref = rest
    else:
        h_ref, act_ref = rest
    x = x_ref[...]
    u = _rms(x, gpre_ref[...]).astype(BF16)
    d_ff = wg_ref.shape[1]
    for c in range(d_ff // ff_chunk):
        sl = slice(c * ff_chunk, (c + 1) * ff_chunk)
        gate = _dot(u, wg_ref[:, sl])
        up = _dot(u, wu_ref[:, sl])
        act_ref[:, sl] = (gate * jax.nn.sigmoid(gate) * up).astype(BF16)
    y = _dot(act_ref[...], wd_ref[...])
    h = x + 0.5 * _rms(y, gpost_ref[...])
    h_ref[...] = h
    if emit_u:
        u_ref[...] = _rms(h, gnext_ref[...]).astype(BF16)


def _ffn(x, g_pre, wg, wu, wd, g_post, g_next=None, *, tm):
    m, d = x.shape
    d_ff = wg.shape[1]
    emit_u = g_next is not None
    row = pl.BlockSpec((tm, d), lambda i: (i, 0))
    in_specs = [row, _const_spec((1, d)), _const_spec((d, d_ff)), _const_spec((d, d_ff)),
                _const_spec((d_ff, d)), _const_spec((1, d))]
    args = [x, g_pre, wg, wu, wd, g_post]
    out_shape = [jax.ShapeDtypeStruct((m, d), F32)]
    out_specs = [row]
    if emit_u:
        in_specs.append(_const_spec((1, d)))
        args.append(g_next)
        out_shape.append(jax.ShapeDtypeStruct((m, d), BF16))
        out_specs.append(row)
    outs = pl.pallas_call(
        functools.partial(_ffn_kernel, ff_chunk=MXU_DIM, emit_u=emit_u),
        grid=(m // tm,),
        in_specs=in_specs,
        out_specs=out_specs,
        out_shape=out_shape,
        scratch_shapes=[pltpu.VMEM((tm, d_ff), BF16)],
        compiler_params=pltpu.CompilerParams(
            dimension_semantics=("arbitrary",), vmem_limit_bytes=VMEM_LIMIT),
    )(*args)
    return outs if emit_u else outs[0]


def _proj_kernel(u_ref, w_ref, wgk1_ref, wgk2_ref, bgk_ref, p_ref, g_ref):
    u = u_ref[...]
    p_ref[...] = _dot(u, w_ref[...])

    @pl.when(pl.program_id(1) == 0)
    def _():
        low = _dot(u, wgk1_ref[...]).astype(BF16)
        zg = _dot(low, wgk2_ref[...]) + bgk_ref[...]
        g_ref[...] = (jnp.minimum(zg, 0.0) - jnp.log(1.0 + jnp.exp(-jnp.abs(zg)))) * (1.0 / GLA_TAU)


def _proj(u, w_main, w_gk1, w_gk2, b_gk, *, tm, tn):
    m, d = u.shape
    n = w_main.shape[1]
    qk = w_gk2.shape[1]
    return pl.pallas_call(
        _proj_kernel,
        grid=(m // tm, n // tn),
        in_specs=[pl.BlockSpec((tm, d), lambda i, j: (i, 0)),
                  pl.BlockSpec((d, tn), lambda i, j: (0, j)),
                  _const_spec(w_gk1.shape), _const_spec(w_gk2.shape), _const_spec((1, qk))],
        out_specs=[pl.BlockSpec((tm, tn), lambda i, j: (i, j)),
                   pl.BlockSpec((tm, qk), lambda i, j: (i, 0))],
        out_shape=[jax.ShapeDtypeStruct((m, n), F32), jax.ShapeDtypeStruct((m, qk), F32)],
        compiler_params=pltpu.CompilerParams(
            dimension_semantics=("arbitrary", "arbitrary"), vmem_limit_bytes=VMEM_LIMIT),
    )(u, w_main, w_gk1, w_gk2, b_gk)


def _kv_t_kernel(wk_ref, wv_ref, u_ref, k_ref, v_ref):
    u = u_ref[...]
    k_ref[...] = _dot_nt(wk_ref[...], u)
    v_ref[...] = _dot_nt(wv_ref[...], u)


def _kv_t(wk_t, wv_t, u_full, *, tp):
    width, d = wk_t.shape
    n_seq, n_pos, _ = u_full.shape
    out = pl.BlockSpec((None, width, tp), lambda b, j: (b, 0, j))
    return pl.pallas_call(
        _kv_t_kernel,
        grid=(n_seq, pl.cdiv(n_pos, tp)),
        in_specs=[_const_spec((width, d)), _const_spec((width, d)),
                  pl.BlockSpec((None, tp, d), lambda b, j: (b, j, 0))],
        out_specs=[out, out],
        out_shape=[jax.ShapeDtypeStruct((n_seq, width, n_pos), F32)] * 2,
        compiler_params=pltpu.CompilerParams(
            dimension_semantics=("arbitrary", "arbitrary"), vmem_limit_bytes=VMEM_LIMIT),
    )(wk_t, wv_t, u_full)


def _gla_local(problems, c):
    rows = lax.broadcasted_iota(jnp.int32, (c, c), 0)
    cols = lax.broadcasted_iota(jnp.int32, (c, c), 1)
    causal = cols <= rows
    tri = causal.astype(BF16)
    ones_c = jnp.ones((c, LANES), BF16)
    parts = [_split3(g) for _, _, _, g in problems]
    cums = [_dot(tri, p[0]) + _dot(tri, p[1]) + _dot(tri, p[2]) for p in parts]
    sums = [_dot_tn(p[0], ones_c) + _dot_tn(p[1], ones_c) + _dot_tn(p[2], ones_c) for p in parts]
    q_ins, k_ins, k_ends, vbs = [], [], [], []
    for (q, k, v, _), b in zip(problems, cums):
        q_ins.append((q * (GLA_DK ** -0.5) * jnp.exp(b)).astype(BF16))
        k_ins.append((k * jnp.exp(-b)).astype(BF16))
        k_ends.append((k * jnp.exp(b[c - 1:c, :] - b)).astype(BF16))
        vbs.append(v.astype(BF16))
    scores = [_dot_nt(qi, ki) for qi, ki in zip(q_ins, k_ins)]
    o_intra = [_dot(jnp.where(causal, a, 0.0).astype(BF16), vb) for a, vb in zip(scores, vbs)]
    d_state = [_dot_tn(ke, vb) for ke, vb in zip(k_ends, vbs)]
    decays = [jnp.concatenate([jnp.exp(bs)] * (GLA_DV // LANES), axis=1) for bs in sums]
    return list(zip(q_ins, o_intra, d_state, decays))


def _gla_kernel(*refs, n_heads, chunk, n_chunks, lead):
    if lead:
        km_ref, vm_ref, gm_ref = refs[:3]
        refs = refs[3:]
    else:
        s0_ref = refs[0]
        refs = refs[1:]
    q_ref, k_ref, v_ref, g_ref, r_ref, gn_ref, o_ref, sfin_ref, s_scr = refs
    step = pl.program_id(2)

    @pl.when(step == 0)
    def _():
        if lead:
            zeros = jnp.zeros((lead, GLA_DK), F32)
            (_, _, d_state, _), = _gla_local([(zeros, km_ref[...], vm_ref[...], gm_ref[...])], lead)
            s_scr[0] = d_state
        else:
            s_scr[...] = s0_ref[...]

    dk = lambda h: slice(h * GLA_DK, (h + 1) * GLA_DK)
    dv = lambda h: slice(h * GLA_DV, (h + 1) * GLA_DV)
    problems = []
    for h in range(n_heads):
        for ci in range(n_chunks):
            sl = slice(ci * chunk, (ci + 1) * chunk)
            problems.append((q_ref[sl, dk(h)], k_ref[sl, dk(h)], v_ref[sl, dv(h)], g_ref[sl, dk(h)]))
    local = _gla_local(problems, chunk)
    gn = gn_ref[...]
    for h in range(n_heads):
        s = s_scr[h]
        for ci in range(n_chunks):
            sl = slice(ci * chunk, (ci + 1) * chunk)
            q_in, o_intra, d_state, decay = local[h * n_chunks + ci]
            o = o_intra + _dot(q_in, s.astype(BF16))
            s = decay * s + d_state
            r = r_ref[sl, dv(h)]
            o_ref[sl, dv(h)] = (_rms(o, gn) * (r * jax.nn.sigmoid(r))).astype(BF16)
        s_scr[h] = s

    @pl.when(step == pl.num_programs(2) - 1)
    def _():
        sfin_ref[...] = s_scr[...]


def _gla(proj, glog, gla_norm_g, *, n_seq, seq_len, rows_per_step, chunk, n_heads, meta=None, state0=None):
    steps = seq_len // rows_per_step
    n_chunks = rows_per_step // chunk
    groups = GLA_HEADS // n_heads
    wk, wv = n_heads * GLA_DK, n_heads * GLA_DV

    def rows(width, col0):
        return pl.BlockSpec((rows_per_step, width), lambda b, h, s: (b * steps + s, col0 // width + h))

    state_spec = pl.BlockSpec((None, n_heads, GLA_DK, GLA_DV), lambda b, h, s: (b, h, 0, 0))
    in_specs, args = [], []
    if meta is not None:
        assert n_heads == 1
        pm, gm = meta
        in_specs += [pl.BlockSpec((N_META, GLA_DK), lambda b, h, s: (0, COL_KA // GLA_DK + h)),
                     pl.BlockSpec((N_META, GLA_DV), lambda b, h, s: (0, COL_VA // GLA_DV + h)),
                     pl.BlockSpec((N_META, GLA_DK), lambda b, h, s: (0, h))]
        args += [pm, pm, gm]
    else:
        in_specs.append(state_spec)
        args.append(state0)
    in_specs += [rows(wk, COL_QA), rows(wk, COL_KA), rows(wv, COL_VA), rows(wk, 0), rows(wv, COL_RA),
                 _const_spec((1, GLA_DV))]
    args += [proj, proj, proj, glog, proj, gla_norm_g]
    m = n_seq * seq_len
    return pl.pallas_call(
        functools.partial(_gla_kernel, n_heads=n_heads, chunk=chunk, n_chunks=n_chunks,
                          lead=N_META if meta is not None else 0),
        grid=(n_seq, groups, steps),
        in_specs=in_specs,
        out_specs=[rows(wv, 0), state_spec],
        out_shape=[jax.ShapeDtypeStruct((m, GLA_HEADS * GLA_DV), BF16),
                   jax.ShapeDtypeStruct((n_seq, GLA_HEADS, GLA_DK, GLA_DV), F32)],
        scratch_shapes=[pltpu.VMEM((n_heads, GLA_DK, GLA_DV), F32)],
        compiler_params=pltpu.CompilerParams(
            dimension_semantics=("arbitrary", "arbitrary", "arbitrary"), vmem_limit_bytes=VMEM_LIMIT),
    )(*args)


def _sb_logs(zb):
    nl = jnp.maximum(zb, 0.0) + jnp.log(1.0 + jnp.exp2(jnp.abs(zb) * (-LOG2E)))
    return zb - nl, nl


def _suffix_matrix(t):
    rows = lax.broadcasted_iota(jnp.int32, (t, t), 0)
    cols = lax.broadcasted_iota(jnp.int32, (t, t), 1)
    upper = (rows > cols).astype(BF16)
    return jnp.concatenate([upper, upper], axis=0)


def _sb_suffix(nl, suffix):
    hi, lo = _split2(nl)
    later = _dot(jnp.concatenate([hi, lo], axis=1), suffix)
    return later, later[:, 0:1] + nl[:, 0:1]


def _own_head_mask(n_rows):
    width = SB_HEADS * SB_HEAD_DIM
    row_id = lax.broadcasted_iota(jnp.int32, (n_rows, width), 0)
    col_id = lax.broadcasted_iota(jnp.int32, (n_rows, width), 1)
    return lax.shift_right_logical(col_id, SB_HEAD_DIM.bit_length() - 1) == (row_id & (SB_HEADS - 1))


def _sb_kernel(vq_ref, vt_ref, pt_ref, bias_ref, q_ref, k_ref, v_ref, km_ref, vm_ref,
               sbias_ref, sq_ref, skn_ref, svn_ref, ck_hbm, cv_hbm,
               o_ref, so_ref,
               q_scr, k_scr, vv_scr, bt_scr, sfx_scr, z_scr, w_scr, later_scr, tot_scr, nrun_scr, acc_scr,
               sqbd_scr, sacc_scr, srun_scr, spage_sfx_scr, kbuf, vbuf, sems,
               *, tile, n_main, n_visits, n_tok, seqs_per_step, n_pages):
    pair = pl.program_id(1)
    low_lanes = lax.broadcasted_iota(jnp.int32, (1, LANES), 1) < SB_HEAD_DIM
    both = lambda x: jnp.concatenate([jnp.where(low_lanes, x, 0.0), jnp.where(low_lanes, 0.0, x)], axis=0)
    DIAG, FULL, META, NONE = range(4)

    for j in range(n_main):
        sl = slice(j * tile, (j + 1) * tile)
        q_scr[j] = both(q_ref[sl, :] * (SB_HEAD_DIM ** -0.5)).astype(BF16)
        k_scr[j] = k_ref[sl, :].astype(BF16)
        vv_scr[j] = both(v_ref[sl, :]).astype(BF16)
    pad = jnp.zeros((tile - LANES, LANES), F32)
    k_scr[n_main] = jnp.concatenate([km_ref[...], pad], axis=0).astype(BF16)
    vv_scr[n_main] = both(jnp.concatenate([vm_ref[...], pad], axis=0)).astype(BF16)
    rows = lax.broadcasted_iota(jnp.int32, (tile, tile), 0)
    cols = lax.broadcasted_iota(jnp.int32, (tile, tile), 1)
    kinds = {DIAG: jnp.where(cols < rows, 0.0, NEG_BIG), FULL: jnp.zeros((tile, tile), F32),
             META: jnp.where(cols < N_META, 0.0, NEG_BIG), NONE: jnp.full((tile, tile), NEG_BIG, F32)}
    for kind, mask in kinds.items():
        for h in range(HEADS_PER_SLAB):
            bt_scr[kind, h * tile:(h + 1) * tile, :] = mask + bias_ref[HEADS_PER_SLAB * pair + h]
    sfx_scr[...] = _suffix_matrix(tile)

    def key_slot(q, t):
        return jnp.where(t <= q, q - t, n_main)

    def kind_of(q, t):
        return jnp.where(t == 0, DIAG, jnp.where(t <= q, FULL, jnp.where(t == q + 1, META, NONE)))

    def logits(v):
        q, t = vq_ref[v], vt_ref[v]
        return _dot_nt(q_scr[q], k_scr[key_slot(q, t)])

    def stage(v, cur, nxt):
        z_scr[nxt] = logits(v + 1)
        w, nl = _sb_logs(z_scr[cur] + bt_scr[kind_of(vq_ref[v], vt_ref[v])])
        later, tot = _sb_suffix(nl, sfx_scr[...])
        prev = jnp.maximum(v - 1, 0)
        q, t = vq_ref[prev], vt_ref[prev]
        fresh = t == 0
        nrun = jnp.where(fresh, 0.0, nrun_scr[...])
        a = jnp.exp(w_scr[nxt] - later_scr[nxt] - jnp.concatenate([nrun, nrun], axis=1)).astype(BF16)
        acc = jnp.where(fresh, 0.0, acc_scr[...]) + _dot(jnp.concatenate([a[:tile], a[tile:]], axis=1),
                                                          vv_scr[key_slot(q, t)])
        acc_scr[...] = acc
        o_ref[pl.ds(pl.multiple_of(q * tile, tile), tile), :] = acc.astype(BF16)
        nrun_scr[...] = nrun + tot_scr[nxt]
        w_scr[cur] = w
        later_scr[cur] = later
        tot_scr[cur] = jnp.broadcast_to(tot, tot_scr.shape[1:])

    z_scr[0] = logits(0)
    w_scr[1] = jnp.full(w_scr.shape[1:], NEG_BIG, F32)
    later_scr[1] = jnp.zeros(later_scr.shape[1:], F32)
    tot_scr[1] = jnp.zeros(tot_scr.shape[1:], F32)
    nrun_scr[...] = jnp.zeros(nrun_scr.shape, F32)
    acc_scr[...] = jnp.zeros(acc_scr.shape, F32)

    n_rows = n_tok * SB_HEADS
    width = SB_HEADS * SB_HEAD_DIM
    own_head = _own_head_mask(n_rows)
    sbias = sbias_ref[...]
    tok_of_row = lax.shift_right_logical(lax.broadcasted_iota(jnp.int32, (n_rows, 1), 0),
                                         SB_HEADS.bit_length() - 1)
    for s in range(seqs_per_step):
        r0 = s * n_tok
        q_rows = jnp.concatenate(
            [jnp.broadcast_to(sq_ref[r0 + t:r0 + t + 1, :] * (SB_HEAD_DIM ** -0.5), (SB_HEADS, width))
             for t in range(n_tok)], axis=0)
        qbd = jnp.where(own_head, q_rows, 0.0).astype(BF16)
        sqbd_scr[s] = qbd
        qf = qbd.astype(F32)
        nrun = jnp.zeros((n_rows, 1), F32)
        acc = jnp.zeros((n_rows, width), F32)
        for j in reversed(range(n_tok)):
            kj = skn_ref[r0 + j:r0 + j + 1, :].astype(BF16).astype(F32)
            vj = svn_ref[r0 + j:r0 + j + 1, :].astype(BF16).astype(F32)
            z = jnp.sum(qf * kj, axis=-1, keepdims=True) + sbias
            w, nl = _sb_logs(jnp.where(tok_of_row > j, z, NEG_BIG))
            acc = acc + jnp.exp(w - nrun) * vj
            nrun = nrun + nl
        sacc_scr[s] = acc
        srun_scr[s] = nrun
    spage_sfx_scr[...] = _suffix_matrix(kbuf.shape[2])

    trips = pl.cdiv(n_visits + 1, SB_VISITS_PER_TRIP)
    per_trip = SB_PAGES_PER_TRIP
    n_total = seqs_per_step * n_pages
    ring = kbuf.shape[0]
    seq0 = (pl.program_id(0) * pl.num_programs(1) + pair) * seqs_per_step

    def page_copies(g):
        g = jnp.asarray(g, jnp.int32)
        gc = jnp.minimum(g, n_total - 1)
        pool = pt_ref[seq0 + lax.div(gc, jnp.int32(n_pages)), n_pages - 1 - lax.rem(gc, jnp.int32(n_pages))]
        slot = lax.rem(g, jnp.int32(ring))
        return (pltpu.make_async_copy(ck_hbm.at[pool], kbuf.at[slot], sems.at[0, slot]),
                pltpu.make_async_copy(cv_hbm.at[pool], vbuf.at[slot], sems.at[1, slot]))

    def start_pages(g0):
        for i in range(per_trip):
            for copy in page_copies(g0 + i):
                copy.start()

    def wait_pages(g0):
        for i in range(per_trip):
            for copy in page_copies(g0 + i):
                copy.wait()

    def page_logits(u):
        g0 = u * per_trip
        seq = lax.div(jnp.minimum(g0, n_total - 1), jnp.int32(n_pages))
        off = sbias + jnp.where(g0 < n_total, 0.0, NEG_BIG)
        return [_dot(sqbd_scr[seq], kbuf[lax.rem(g0 + i, jnp.int32(ring))].astype(BF16)) + off
                for i in range(per_trip)]

    def page_suffix(zs):
        logs = [_sb_logs(z) for z in zs]
        sums = [_sb_suffix(nl, spage_sfx_scr[...]) for _, nl in logs]
        return [(w, later, tot) for (w, _), (later, tot) in zip(logs, sums)]

    def page_finish(u, parts):
        g0 = u * per_trip
        seq = lax.div(jnp.minimum(g0, n_total - 1), jnp.int32(n_pages))
        nrun = srun_scr[seq]
        acc = sacc_scr[seq]
        for i, (w, later, tot) in enumerate(parts):
            a = jnp.exp(w - later - nrun).astype(BF16)
            acc = acc + _dot_nt(a, vbuf[lax.rem(g0 + i, jnp.int32(ring))].astype(BF16))
            nrun = nrun + tot
        sacc_scr[seq] = acc
        srun_scr[seq] = nrun

    for ahead in range(SB_PREFETCH_TRIPS):
        start_pages(ahead * per_trip)

    def body(u, carry):
        wait_pages(u * per_trip)
        start_pages((u + SB_PREFETCH_TRIPS) * per_trip)
        v0 = SB_VISITS_PER_TRIP * u
        zs = page_logits(u)
        stage(v0, 0, 1)
        parts = page_suffix(zs)
        stage(v0 + 1, 1, 0)
        page_finish(u, parts)
        for i in range(2, SB_VISITS_PER_TRIP):
            stage(v0 + i, i % 2, (i + 1) % 2)
        return carry

    lax.fori_loop(0, trips, body, 0)
    for ahead in range(SB_PREFETCH_TRIPS):
        wait_pages((trips + ahead) * per_trip)

    for s in range(seqs_per_step):
        picked = jnp.where(own_head, sacc_scr[s], 0.0)
        for t in range(n_tok):
            so_ref[s * n_tok + t:s * n_tok + t + 1, :] = jnp.sum(
                picked[t * SB_HEADS:(t + 1) * SB_HEADS, :], axis=0, keepdims=True)


def _sb_attention(proj, proj_meta, proj_s, cache_k, cache_v, page_table, sb_bias, *, n_seq, seq_len, tile,
                  n_dec, n_tok):
    p3 = proj.reshape(n_seq, seq_len, PROJ_W)
    slab = lambda col: col // LANES
    n_main = seq_len // tile
    pair_rows = HEADS_PER_SLAB * tile
    n_pairs = SB_HEADS // HEADS_PER_SLAB
    width = SB_HEADS * SB_HEAD_DIM
    n_pages = page_table.shape[1]
    page_size = cache_k.shape[1]
    seqs_per_step = n_dec // (n_seq * n_pairs)
    assert seqs_per_step * n_seq * n_pairs == n_dec and n_pages % SB_PAGES_PER_TRIP == 0
    step_rows = seqs_per_step * n_tok
    n_rows = n_tok * SB_HEADS
    ck = cache_k.transpose(0, 2, 3, 1).reshape(cache_k.shape[0], width, page_size)
    cv = cache_v.transpose(0, 2, 3, 1).reshape(cache_v.shape[0], width, page_size)
    bias_rows = jnp.tile(sb_bias, n_tok).reshape(n_rows, 1)
    ring = (SB_PREFETCH_TRIPS + 1) * SB_PAGES_PER_TRIP
    visits = [(q, t) for q in range(n_main) for t in range(q + 2)]
    n_visits = len(visits)
    visits += [(n_main - 1, n_main + 1)] * (SB_VISITS_PER_TRIP + 2)
    vq = jnp.asarray([q for q, _ in visits], jnp.int32)
    vt = jnp.asarray([t for _, t in visits], jnp.int32)
    assert seqs_per_step * n_pages <= pl.cdiv(n_visits + 1, SB_VISITS_PER_TRIP) * SB_PAGES_PER_TRIP
    seq_blk = lambda col: pl.BlockSpec((None, seq_len, LANES), lambda b, p, *_: (b, 0, slab(col) + p))
    meta_blk = lambda col: pl.BlockSpec((LANES, LANES), lambda b, p, *_: (0, slab(col) + p))
    new_blk = lambda col: pl.BlockSpec((step_rows, width), lambda b, p, *_: (b * n_pairs + p, col // width))
    grid_spec = pltpu.PrefetchScalarGridSpec(
        num_scalar_prefetch=3,
        grid=(n_seq, n_pairs),
        in_specs=[pl.BlockSpec(memory_space=pltpu.SMEM),
                  seq_blk(COL_QB), seq_blk(COL_KB), seq_blk(COL_VB), meta_blk(COL_KB), meta_blk(COL_VB),
                  pl.BlockSpec((n_rows, 1), lambda b, p, *_: (0, 0)),
                  new_blk(COL_QB), new_blk(COL_KB), new_blk(COL_VB),
                  pl.BlockSpec(memory_space=pl.ANY), pl.BlockSpec(memory_space=pl.ANY)],
        out_specs=[pl.BlockSpec((None, seq_len, LANES), lambda b, p, *_: (b, 0, p)),
                   pl.BlockSpec((step_rows, width), lambda b, p, *_: (b * n_pairs + p, 0))],
        scratch_shapes=[pltpu.VMEM((n_main, pair_rows, LANES), BF16),
                        pltpu.VMEM((n_main + 1, tile, LANES), BF16),
                        pltpu.VMEM((n_main + 1, pair_rows, LANES), BF16),
                        pltpu.VMEM((4, pair_rows, tile), F32),
                        pltpu.VMEM((2 * tile, tile), BF16),
                        pltpu.VMEM((2, pair_rows, tile), F32),
                        pltpu.VMEM((2, pair_rows, tile), F32),
                        pltpu.VMEM((2, pair_rows, tile), F32),
                        pltpu.VMEM((2, pair_rows, LANES), F32),
                        pltpu.VMEM((pair_rows, LANES), F32),
                        pltpu.VMEM((tile, LANES), F32),
                        pltpu.VMEM((seqs_per_step, n_rows, width), BF16),
                        pltpu.VMEM((seqs_per_step, n_rows, width), F32),
                        pltpu.VMEM((seqs_per_step, n_rows, 1), F32),
                        pltpu.VMEM((2 * page_size, page_size), BF16),
                        pltpu.VMEM((ring, width, page_size), F32),
                        pltpu.VMEM((ring, width, page_size), F32),
                        pltpu.SemaphoreType.DMA((2, ring))],
    )
    out_p, out_s = pl.pallas_call(
        functools.partial(_sb_kernel, tile=tile, n_main=n_main, n_visits=n_visits, n_tok=n_tok,
                          seqs_per_step=seqs_per_step, n_pages=n_pages),
        grid_spec=grid_spec,
        out_shape=[jax.ShapeDtypeStruct((n_seq, seq_len, width), BF16),
                   jax.ShapeDtypeStruct((n_dec * n_tok, width), F32)],
        compiler_params=pltpu.CompilerParams(
            dimension_semantics=("arbitrary", "arbitrary"), vmem_limit_bytes=VMEM_LIMIT),
    )(vq, vt, page_table, sb_bias, p3, p3, p3, proj_meta, proj_meta, bias_rows, proj_s, proj_s, proj_s, ck, cv)
    return out_p.reshape(n_seq * seq_len, width), out_s


def _merge_kernel(oa_ref, ob_ref, ga_ref, gb_ref, h_ref, woa_ref, wob_ref, wout_ref, g_ref, out_ref):
    pa = _dot(oa_ref[...].astype(BF16), woa_ref[...])
    pb = _dot(ob_ref[...].astype(BF16), wob_ref[...])
    mixed = jax.nn.sigmoid(ga_ref[...]) * pa + jax.nn.sigmoid(gb_ref[...]) * pb
    out_ref[...] = h_ref[...] + _rms(_dot(mixed.astype(BF16), wout_ref[...]), g_ref[...])


def _merge(o_a, o_b, proj, h, w_o_gla, w_o_sb, w_out, g_post, *, tm):
    m, d = h.shape
    row = pl.BlockSpec((tm, d), lambda i: (i, 0))
    return pl.pallas_call(
        _merge_kernel,
        grid=(m // tm,),
        in_specs=[row, row,
                  pl.BlockSpec((tm, d), lambda i: (i, COL_GA // d)),
                  pl.BlockSpec((tm, d), lambda i: (i, COL_GB // d)),
                  row, _const_spec(w_o_gla.shape), _const_spec(w_o_sb.shape), _const_spec(w_out.shape),
                  _const_spec((1, d))],
        out_specs=row,
        out_shape=jax.ShapeDtypeStruct((m, d), F32),
        compiler_params=pltpu.CompilerParams(
            dimension_semantics=("arbitrary",), vmem_limit_bytes=VMEM_LIMIT),
    )(o_a, o_b, proj, proj, h, w_o_gla, w_o_sb, w_out, g_post)


def _row_tile(m, cap):
    return m if m <= cap else cap


def kernel(x_prompt, x_sample, cache_k, cache_v, state_gla, page_table, meta_tokens,
           ffn1_pre_g, ffn1_w_gate, ffn1_w_up, ffn1_w_down, ffn1_post_g,
           mix_pre_g, w_in, w_gk2, b_gk, gla_norm_g, sb_bias, w_o_gla, w_o_sb, w_out, mix_post_g,
           ffn2_pre_g, ffn2_w_gate, ffn2_w_up, ffn2_w_down, ffn2_post_g):
    n_seq, seq_len, d = x_prompt.shape
    n_dec, n_tok, _ = x_sample.shape
    assert ffn1_pre_g.shape[0] == 1, "single layer"
    qk = GLA_HEADS * GLA_DK
    v_w = GLA_HEADS * GLA_DV
    rank = w_gk2.shape[1]

    bf = lambda w: w[0].astype(BF16)
    w1g, w1u, w1d = bf(ffn1_w_gate), bf(ffn1_w_up), bf(ffn1_w_down)
    w2g, w2u, w2d = bf(ffn2_w_gate), bf(ffn2_w_up), bf(ffn2_w_down)
    w_in0 = w_in[0]
    gk0 = 2 * qk + v_w
    w_main = jnp.concatenate([w_in0[:, :gk0], w_in0[:, gk0 + rank:]], axis=1).astype(BF16)
    w_gk1 = jnp.pad(w_in0[:, gk0:gk0 + rank], ((0, 0), (0, LANES - rank))).astype(BF16)
    w_gk2p = jnp.pad(w_gk2[0], ((0, LANES - rank), (0, 0))).astype(BF16)
    woa, wob, wout = bf(w_o_gla), bf(w_o_sb), bf(w_out)
    row = lambda g: g[0].reshape(1, -1)
    bias = sb_bias[0]

    def pre_mix(x, tm):
        h, u = _ffn(x, row(ffn1_pre_g), w1g, w1u, w1d, row(ffn1_post_g), row(mix_pre_g), tm=tm)
        proj, glog = _proj(u, w_main, w_gk1, w_gk2p, row(b_gk), tm=_row_tile(x.shape[0], 1024), tn=1024)
        return h, u, proj, glog

    def post_mix(h, proj, o_a, o_b, tm):
        h2 = _merge(o_a, o_b, proj, h, woa, wob, wout, row(mix_post_g), tm=tm)
        return _ffn(h2, row(ffn2_pre_g), w2g, w2u, w2d, row(ffn2_post_g), tm=tm)

    x_meta = jnp.pad(meta_tokens.astype(F32), ((0, LANES - N_META), (0, 0)))
    _, u_m, proj_m, glog_m = pre_mix(x_meta, LANES)

    m_p = n_seq * seq_len
    tm_p = _row_tile(m_p, 512)
    h_p, u_p, proj_p, glog_p = pre_mix(x_prompt.reshape(m_p, d), tm_p)
    oa_p, s_p = _gla(proj_p, glog_p, row(gla_norm_g), n_seq=n_seq, seq_len=seq_len,
                     rows_per_step=_row_tile(seq_len, 512), chunk=GLA_CHUNK, n_heads=1,
                     meta=(proj_m, glog_m))

    m_s = n_dec * n_tok
    tm_s = _row_tile(m_s, 512)
    h_s, _, proj_s, glog_s = pre_mix(x_sample.reshape(m_s, d), tm_s)

    ob_p, ob_s = _sb_attention(proj_p, proj_m, proj_s, cache_k[0], cache_v[0], page_table, bias,
                               n_seq=n_seq, seq_len=seq_len, tile=MXU_DIM, n_dec=n_dec, n_tok=n_tok)
    y_p = post_mix(h_p, proj_p, oa_p, ob_p, tm_p)

    pad_tok = lambda a: jnp.pad(a.reshape(n_dec, n_tok, -1), ((0, 0), (0, SUBLANES - n_tok), (0, 0))
                                ).reshape(n_dec * SUBLANES, -1)
    oa_s, s_s = _gla(pad_tok(proj_s[:, :COL_QB]), pad_tok(glog_s), row(gla_norm_g), n_seq=n_dec,
                     seq_len=SUBLANES, rows_per_step=SUBLANES, chunk=SUBLANES, n_heads=GLA_HEADS,
                     state0=state_gla[0])
    oa_s = oa_s.reshape(n_dec, SUBLANES, v_w)[:, :n_tok].reshape(m_s, v_w)
    y_s = post_mix(h_s, proj_s, oa_s, ob_s, tm_s)

    heads = (SB_HEADS, SB_HEAD_DIM)
    w_sb = SB_HEADS * SB_HEAD_DIM

    u_full = jnp.concatenate([jnp.broadcast_to(u_m[:N_META][None], (n_seq, N_META, d)),
                              u_p.reshape(n_seq, seq_len, d)], axis=1)
    k_t, v_t = _kv_t(w_main[:, COL_KB:COL_KB + w_sb].T, w_main[:, COL_VB:COL_VB + w_sb].T, u_full,
                     tp=512)
    kv_out = lambda a: a.reshape(n_seq, *heads, N_META + seq_len).transpose(0, 3, 1, 2)[None]

    return (y_p.reshape(n_seq, seq_len, d),
            y_s.reshape(n_dec, n_tok, d),
            kv_out(k_t), kv_out(v_t),
            s_p[None],
            proj_s[:, COL_KB:COL_KB + w_sb].reshape(1, n_dec, n_tok, *heads),
            proj_s[:, COL_VB:COL_VB + w_sb].reshape(1, n_dec, n_tok, *heads),
            s_s[None])
```

```python
import functools

import jax
import jax.numpy as jnp
from jax import lax
from jax.experimental import pallas as pl
from jax.experimental.pallas import tpu as pltpu

F32 = jnp.float32
BF16 = jnp.bfloat16

EPS = 1e-6
N_META = 16
GLA_HEADS = 4
GLA_DK = 128
GLA_DV = 256
GLA_TAU = 16.0
GLA_CHUNK = 64
SB_HEADS = 16
SB_HEAD_DIM = 64
LANES = 128
SUBLANES = 8
MXU_DIM = 256
HEADS_PER_SLAB = LANES // SB_HEAD_DIM
VMEM_LIMIT = 56 * 1024 * 1024
NEG_BIG = -1e30
LOG2E = 1.4426950408889634
SB_VISITS_PER_TRIP = 4
SB_PAGES_PER_TRIP = 2
SB_PREFETCH_TRIPS = 4

COL_QA, COL_KA, COL_VA, COL_RA = 0, 512, 1024, 2048
COL_QB, COL_KB, COL_VB, COL_GA, COL_GB = 3072, 4096, 5120, 6144, 7168
PROJ_W = 8192


def _rms(x, g):
    return x * lax.rsqrt(jnp.mean(x * x, axis=-1, keepdims=True) + EPS) * g


def _const_spec(shape):
    return pl.BlockSpec(shape, lambda *_: (0,) * len(shape), pipeline_mode=pl.Buffered(1))


def _dot(a, b):
    return jnp.dot(a, b, preferred_element_type=F32)


def _dot_nt(a, b):
    return lax.dot_general(a, b, (((1,), (1,)), ((), ())), preferred_element_type=F32)


def _dot_tn(a, b):
    return lax.dot_general(a, b, (((0,), (0,)), ((), ())), preferred_element_type=F32)


def _split3(x):
    hi = x.astype(BF16)
    r1 = x - hi.astype(F32)
    mid = r1.astype(BF16)
    lo = (r1 - mid.astype(F32)).astype(BF16)
    return hi, mid, lo


def _ffn_kernel(x_ref, gpre_ref, wg_ref, wu_ref, wd_ref, gpost_ref, *rest, ff_chunk, emit_u):
    if emit_u:
        gnext_ref, h_ref, u_ref, act_ref = rest
    else:
        h_ref, act_ref = rest
    x = x_ref[...]
    u = _rms(x, gpre_ref[...]).astype(BF16)
    d_ff = wg_ref.shape[1]
    for c in range(d_ff // ff_chunk):
        sl = slice(c * ff_chunk, (c + 1) * ff_chunk)
        gate = _dot(u, wg_ref[:, sl])
        up = _dot(u, wu_ref[:, sl])
        act_ref[:, sl] = (gate * jax.nn.sigmoid(gate) * up).astype(BF16)
    y = _dot(act_ref[...], wd_ref[...])
    h = x + 0.5 * _rms(y, gpost_ref[...])
    h_ref[...] = h
    if emit_u:
        u_ref[...] = _rms(h, gnext_ref[...]).astype(BF16)


def _ffn(x, g_pre, wg, wu, wd, g_post, g_next=None, *, tm):
    m, d = x.shape
    d_ff = wg.shape[1]
    emit_u = g_next is not None
    row = pl.BlockSpec((tm, d), lambda i: (i, 0))
    in_specs = [row, _const_spec((1, d)), _const_spec((d, d_ff)), _const_spec((d, d_ff)),
                _const_spec((d_ff, d)), _const_spec((1, d))]
    args = [x, g_pre, wg, wu, wd, g_post]
    out_shape = [jax.ShapeDtypeStruct((m, d), F32)]
    out_specs = [row]
    if emit_u:
        in_specs.append(_const_spec((1, d)))
        args.append(g_next)
        out_shape.append(jax.ShapeDtypeStruct((m, d), BF16))
        out_specs.append(row)
    outs = pl.pallas_call(
        functools.partial(_ffn_kernel, ff_chunk=MXU_DIM, emit_u=emit_u),
        grid=(m // tm,),
        in_specs=in_specs,
        out_specs=out_specs,
        out_shape=out_shape,
        scratch_shapes=[pltpu.VMEM((tm, d_ff), BF16)],
        compiler_params=pltpu.CompilerParams(
            dimension_semantics=("arbitrary",), vmem_limit_bytes=VMEM_LIMIT),
    )(*args)
    return outs if emit_u else outs[0]


def _proj_kernel(u_ref, w_ref, wgk1_ref, wgk2_ref, bgk_ref, p_ref, g_ref):
    u = u_ref[...]
    p_ref[...] = _dot(u, w_ref[...])

    @pl.when(pl.program_id(1) == 0)
    def _():
        low = _dot(u, wgk1_ref[...]).astype(BF16)
        zg = _dot(low, wgk2_ref[...]) + bgk_ref[...]
        g_ref[...] = (jnp.minimum(zg, 0.0) - jnp.log(1.0 + jnp.exp(-jnp.abs(zg)))) * (1.0 / GLA_TAU)


def _proj(u, w_main, w_gk1, w_gk2, b_gk, *, tm, tn):
    m, d = u.shape
    n = w_main.shape[1]
    qk = w_gk2.shape[1]
    return pl.pallas_call(
        _proj_kernel,
        grid=(m // tm, n // tn),
        in_specs=[pl.BlockSpec((tm, d), lambda i, j: (i, 0)),
                  pl.BlockSpec((d, tn), lambda i, j: (0, j)),
                  _const_spec(w_gk1.shape), _const_spec(w_gk2.shape), _const_spec((1, qk))],
        out_specs=[pl.BlockSpec((tm, tn), lambda i, j: (i, j)),
                   pl.BlockSpec((tm, qk), lambda i, j: (i, 0))],
        out_shape=[jax.ShapeDtypeStruct((m, n), F32), jax.ShapeDtypeStruct((m, qk), F32)],
        compiler_params=pltpu.CompilerParams(
            dimension_semantics=("arbitrary", "arbitrary"), vmem_limit_bytes=VMEM_LIMIT),
    )(u, w_main, w_gk1, w_gk2, b_gk)


def _kv_t_kernel(wk_ref, wv_ref, u_ref, k_ref, v_ref):
    u = u_ref[...]
    k_ref[...] = _dot_nt(wk_ref[...], u)
    v_ref[...] = _dot_nt(wv_ref[...], u)


def _kv_t(wk_t, wv_t, u_full, *, tp):
    width, d = wk_t.shape
    n_seq, n_pos, _ = u_full.shape
    out = pl.BlockSpec((None, width, tp), lambda b, j: (b, 0, j))
    return pl.pallas_call(
        _kv_t_kernel,
        grid=(n_seq, pl.cdiv(n_pos, tp)),
        in_specs=[_const_spec((width, d)), _const_spec((width, d)),
                  pl.BlockSpec((None, tp, d), lambda b, j: (b, j, 0))],
        out_specs=[out, out],
        out_shape=[jax.ShapeDtypeStruct((n_seq, width, n_pos), F32)] * 2,
        compiler_params=pltpu.CompilerParams(
            dimension_semantics=("arbitrary", "arbitrary"), vmem_limit_bytes=VMEM_LIMIT),
    )(wk_t, wv_t, u_full)


def _gla_local(problems, c):
    rows = lax.broadcasted_iota(jnp.int32, (c, c), 0)
    cols = lax.broadcasted_iota(jnp.int32, (c, c), 1)
    causal = cols <= rows
    tri = causal.astype(BF16)
    ones_c = jnp.ones((c, LANES), BF16)
    parts = [_split3(g) for _, _, _, g in problems]
    cums = [_dot(tri, p[0]) + _dot(tri, p[1]) + _dot(tri, p[2]) for p in parts]
    sums = [_dot_tn(p[0], ones_c) + _dot_tn(p[1], ones_c) + _dot_tn(p[2], ones_c) for p in parts]
    q_ins, k_ins, k_ends, vbs = [], [], [], []
    for (q, k, v, _), b in zip(problems, cums):
        q_ins.append((q * (GLA_DK ** -0.5) * jnp.exp(b)).astype(BF16))
        k_ins.append((k * jnp.exp(-b)).astype(BF16))
        k_ends.append((k * jnp.exp(b[c - 1:c, :] - b)).astype(BF16))
        vbs.append(v.astype(BF16))
    scores = [_dot_nt(qi, ki) for qi, ki in zip(q_ins, k_ins)]
    o_intra = [_dot(jnp.where(causal, a, 0.0).astype(BF16), vb) for a, vb in zip(scores, vbs)]
    d_state = [_dot_tn(ke, vb) for ke, vb in zip(k_ends, vbs)]
    decays = [jnp.concatenate([jnp.exp(bs)] * (GLA_DV // LANES), axis=1) for bs in sums]
    return list(zip(q_ins, o_intra, d_state, decays))


def _gla_kernel(*refs, n_heads, chunk, n_chunks, lead):
    if lead:
        km_ref, vm_ref, gm_ref = refs[:3]
        refs = refs[3:]
    else:
        s0_ref = refs[0]
        refs = refs[1:]
    q_ref, k_ref, v_ref, g_ref, r_ref, gn_ref, o_ref, sfin_ref, s_scr = refs
    step = pl.program_id(2)
    dk = lambda h: slice(h * GLA_DK, (h + 1) * GLA_DK)
    dv = lambda h: slice(h * GLA_DV, (h + 1) * GLA_DV)

    @pl.when(step == 0)
    def _():
        if lead:
            zeros = jnp.zeros((lead, GLA_DK), F32)
            metas = _gla_local([(zeros, km_ref[:, dk(h)], vm_ref[:, dv(h)], gm_ref[:, dk(h)])
                                for h in range(n_heads)], lead)
            for h in range(n_heads):
                s_scr[h] = metas[h][2]
        else:
            s_scr[...] = s0_ref[...]

    problems = []
    for h in range(n_heads):
        for ci in range(n_chunks):
            sl = slice(ci * chunk, (ci + 1) * chunk)
            problems.append((q_ref[sl, dk(h)], k_ref[sl, dk(h)], v_ref[sl, dv(h)], g_ref[sl, dk(h)]))
    local = _gla_local(problems, chunk)
    gn = gn_ref[...]
    for h in range(n_heads):
        s = s_scr[h]
        for ci in range(n_chunks):
            sl = slice(ci * chunk, (ci + 1) * chunk)
            q_in, o_intra, d_state, decay = local[h * n_chunks + ci]
            o = o_intra + _dot(q_in, s.astype(BF16))
            s = decay * s + d_state
            r = r_ref[sl, dv(h)]
            o_ref[sl, dv(h)] = (_rms(o, gn) * (r * jax.nn.sigmoid(r))).astype(BF16)
        s_scr[h] = s

    @pl.when(step == pl.num_programs(2) - 1)
    def _():
        sfin_ref[...] = s_scr[...]


def _gla(proj, glog, gla_norm_g, *, n_seq, seq_len, rows_per_step, chunk, n_heads, meta=None, state0=None):
    steps = seq_len // rows_per_step
    n_chunks = rows_per_step // chunk
    groups = GLA_HEADS // n_heads
    wk, wv = n_heads * GLA_DK, n_heads * GLA_DV

    def rows(width, col0):
        return pl.BlockSpec((rows_per_step, width), lambda b, h, s: (b * steps + s, col0 // width + h))

    state_spec = pl.BlockSpec((None, n_heads, GLA_DK, GLA_DV), lambda b, h, s: (b, h, 0, 0))
    in_specs, args = [], []
    if meta is not None:
        pm, gm = meta
        in_specs += [pl.BlockSpec((N_META, wk), lambda b, h, s: (0, COL_KA // wk + h)),
                     pl.BlockSpec((N_META, wv), lambda b, h, s: (0, COL_VA // wv + h)),
                     pl.BlockSpec((N_META, wk), lambda b, h, s: (0, h))]
        args += [pm, pm, gm]
    else:
        in_specs.append(state_spec)
        args.append(state0)
    in_specs += [rows(wk, COL_QA), rows(wk, COL_KA), rows(wv, COL_VA), rows(wk, 0), rows(wv, COL_RA),
                 _const_spec((1, GLA_DV))]
    args += [proj, proj, proj, glog, proj, gla_norm_g]
    m = n_seq * seq_len
    return pl.pallas_call(
        functools.partial(_gla_kernel, n_heads=n_heads, chunk=chunk, n_chunks=n_chunks,
                          lead=N_META if meta is not None else 0),
        grid=(n_seq, groups, steps),
        in_specs=in_specs,
        out_specs=[rows(wv, 0), state_spec],
        out_shape=[jax.ShapeDtypeStruct((m, GLA_HEADS * GLA_DV), BF16),
                   jax.ShapeDtypeStruct((n_seq, GLA_HEADS, GLA_DK, GLA_DV), F32)],
        scratch_shapes=[pltpu.VMEM((n_heads, GLA_DK, GLA_DV), F32)],
        compiler_params=pltpu.CompilerParams(
            dimension_semantics=("arbitrary", "arbitrary", "arbitrary"), vmem_limit_bytes=VMEM_LIMIT),
    )(*args)


def _sb_logs(zb):
    nl = jnp.maximum(zb, 0.0) + jnp.log(1.0 + jnp.exp2(jnp.abs(zb) * (-LOG2E)))
    return zb - nl, nl


def _suffix_matrix(t):
    rows = lax.broadcasted_iota(jnp.int32, (t, t), 0)
    cols = lax.broadcasted_iota(jnp.int32, (t, t), 1)
    return (rows > cols).astype(BF16)


def _sb_suffix(nl, suffix):
    later = _dot(nl.astype(BF16), suffix)
    return later, jnp.sum(nl, axis=1, keepdims=True)


def _own_head_mask(n_rows):
    width = SB_HEADS * SB_HEAD_DIM
    row_id = lax.broadcasted_iota(jnp.int32, (n_rows, width), 0)
    col_id = lax.broadcasted_iota(jnp.int32, (n_rows, width), 1)
    return lax.shift_right_logical(col_id, SB_HEAD_DIM.bit_length() - 1) == (row_id & (SB_HEADS - 1))


def _sb_kernel(vq_ref, vt_ref, pt_ref, bias_ref, q_ref, k_ref, v_ref, km_ref, vm_ref,
               sbias_ref, sq_ref, skn_ref, svn_ref, ck_hbm, cv_hbm,
               o_ref, so_ref,
               q_scr, k_scr, vv_scr, bt_scr, sfx_scr, z_scr, w_scr, later_scr, tot_scr, nrun_scr, acc_scr,
               sqbd_scr, sacc_scr, srun_scr, spage_sfx_scr, kbuf, vbuf, sems,
               *, tile, n_main, n_visits, n_tok, seqs_per_step, n_pages):
    pair = pl.program_id(1)
    low_lanes = lax.broadcasted_iota(jnp.int32, (1, LANES), 1) < SB_HEAD_DIM
    both = lambda x: jnp.concatenate([jnp.where(low_lanes, x, 0.0), jnp.where(low_lanes, 0.0, x)], axis=0)
    DIAG, FULL, META, NONE = range(4)

    for j in range(n_main):
        sl = slice(j * tile, (j + 1) * tile)
        q_scr[j] = both(q_ref[sl, :] * (SB_HEAD_DIM ** -0.5)).astype(BF16)
        k_scr[j] = k_ref[sl, :].astype(BF16)
        vv_scr[j] = both(v_ref[sl, :]).astype(BF16)
    pad = jnp.zeros((tile - LANES, LANES), F32)
    k_scr[n_main] = jnp.concatenate([km_ref[...], pad], axis=0).astype(BF16)
    vv_scr[n_main] = both(jnp.concatenate([vm_ref[...], pad], axis=0)).astype(BF16)
    rows = lax.broadcasted_iota(jnp.int32, (tile, tile), 0)
    cols = lax.broadcasted_iota(jnp.int32, (tile, tile), 1)
    kinds = {DIAG: jnp.where(cols < rows, 0.0, NEG_BIG), FULL: jnp.zeros((tile, tile), F32),
             META: jnp.where(cols < N_META, 0.0, NEG_BIG), NONE: jnp.full((tile, tile), NEG_BIG, F32)}
    for kind, mask in kinds.items():
        for h in range(HEADS_PER_SLAB):
            bt_scr[kind, h * tile:(h + 1) * tile, :] = mask + bias_ref[HEADS_PER_SLAB * pair + h]
    sfx_scr[...] = _suffix_matrix(tile)

    def key_slot(q, t):
        return jnp.where(t <= q, q - t, n_main)

    def kind_of(q, t):
        return jnp.where(t == 0, DIAG, jnp.where(t <= q, FULL, jnp.where(t == q + 1, META, NONE)))

    def logits(v):
        q, t = vq_ref[v], vt_ref[v]
        return _dot_nt(q_scr[q], k_scr[key_slot(q, t)])

    def stage(v, cur, nxt):
        z_scr[nxt] = logits(v + 1)
        w, nl = _sb_logs(z_scr[cur] + bt_scr[kind_of(vq_ref[v], vt_ref[v])])
        later, tot = _sb_suffix(nl, sfx_scr[...])
        prev = jnp.maximum(v - 1, 0)
        q, t = vq_ref[prev], vt_ref[prev]
        fresh = t == 0
        nrun = jnp.where(fresh, 0.0, nrun_scr[...])
        a = jnp.exp(w_scr[nxt] - later_scr[nxt] - jnp.concatenate([nrun, nrun], axis=1)).astype(BF16)
        acc = jnp.where(fresh, 0.0, acc_scr[...]) + _dot(jnp.concatenate([a[:tile], a[tile:]], axis=1),
                                                          vv_scr[key_slot(q, t)])
        acc_scr[...] = acc
        o_ref[pl.ds(pl.multiple_of(q * tile, tile), tile), :] = acc.astype(BF16)
        nrun_scr[...] = nrun + tot_scr[nxt]
        w_scr[cur] = w
        later_scr[cur] = later
        tot_scr[cur] = jnp.broadcast_to(tot, tot_scr.shape[1:])

    z_scr[0] = logits(0)
    w_scr[1] = jnp.full(w_scr.shape[1:], NEG_BIG, F32)
    later_scr[1] = jnp.zeros(later_scr.shape[1:], F32)
    tot_scr[1] = jnp.zeros(tot_scr.shape[1:], F32)
    nrun_scr[...] = jnp.zeros(nrun_scr.shape, F32)
    acc_scr[...] = jnp.zeros(acc_scr.shape, F32)

    n_rows = n_tok * SB_HEADS
    width = SB_HEADS * SB_HEAD_DIM
    own_head = _own_head_mask(n_rows)
    sbias = sbias_ref[...]
    tok_of_row = lax.shift_right_logical(lax.broadcasted_iota(jnp.int32, (n_rows, 1), 0),
                                         SB_HEADS.bit_length() - 1)
    for s in range(seqs_per_step):
        r0 = s * n_tok
        q_rows = jnp.concatenate(
            [jnp.broadcast_to(sq_ref[r0 + t:r0 + t + 1, :] * (SB_HEAD_DIM ** -0.5), (SB_HEADS, width))
             for t in range(n_tok)], axis=0)
        qbd = jnp.where(own_head, q_rows, 0.0).astype(BF16)
        sqbd_scr[s] = qbd
        qf = qbd.astype(F32)
        nrun = jnp.zeros((n_rows, 1), F32)
        acc = jnp.zeros((n_rows, width), F32)
        for j in reversed(range(n_tok)):
            kj = skn_ref[r0 + j:r0 + j + 1, :].astype(BF16).astype(F32)
            vj = svn_ref[r0 + j:r0 + j + 1, :].astype(BF16).astype(F32)
            z = jnp.sum(qf * kj, axis=-1, keepdims=True) + sbias
            w, nl = _sb_logs(jnp.where(tok_of_row > j, z, NEG_BIG))
            acc = acc + jnp.exp(w - nrun) * vj
            nrun = nrun + nl
        sacc_scr[s] = acc
        srun_scr[s] = nrun
    spage_sfx_scr[...] = _suffix_matrix(kbuf.shape[2])

    trips = pl.cdiv(n_visits + 1, SB_VISITS_PER_TRIP)
    per_trip = SB_PAGES_PER_TRIP
    n_total = seqs_per_step * n_pages
    ring = kbuf.shape[0]
    seq0 = (pl.program_id(0) * pl.num_programs(1) + pair) * seqs_per_step

    def page_copies(g):
        g = jnp.asarray(g, jnp.int32)
        gc = jnp.minimum(g, n_total - 1)
        pool = pt_ref[seq0 + lax.div(gc, jnp.int32(n_pages)), n_pages - 1 - lax.rem(gc, jnp.int32(n_pages))]
        slot = lax.rem(g, jnp.int32(ring))
        return (pltpu.make_async_copy(ck_hbm.at[pool], kbuf.at[slot], sems.at[0, slot]),
                pltpu.make_async_copy(cv_hbm.at[pool], vbuf.at[slot], sems.at[1, slot]))

    def start_pages(g0):
        for i in range(per_trip):
            for copy in page_copies(g0 + i):
                copy.start()

    def wait_pages(g0):
        for i in range(per_trip):
            for copy in page_copies(g0 + i):
                copy.wait()

    def page_logits(u):
        g0 = u * per_trip
        seq = lax.div(jnp.minimum(g0, n_total - 1), jnp.int32(n_pages))
        off = sbias + jnp.where(g0 < n_total, 0.0, NEG_BIG)
        return [_dot(sqbd_scr[seq], kbuf[lax.rem(g0 + i, jnp.int32(ring))].astype(BF16)) + off
                for i in range(per_trip)]

    def page_suffix(zs):
        logs = [_sb_logs(z) for z in zs]
        sums = [_sb_suffix(nl, spage_sfx_scr[...]) for _, nl in logs]
        return [(w, later, tot) for (w, _), (later, tot) in zip(logs, sums)]

    def page_finish(u, parts):
        g0 = u * per_trip
        seq = lax.div(jnp.minimum(g0, n_total - 1), jnp.int32(n_pages))
        nrun = srun_scr[seq]
        acc = sacc_scr[seq]
        for i, (w, later, tot) in enumerate(parts):
            a = jnp.exp(w - later - nrun).astype(BF16)
            acc = acc + _dot_nt(a, vbuf[lax.rem(g0 + i, jnp.int32(ring))].astype(BF16))
            nrun = nrun + tot
        sacc_scr[seq] = acc
        srun_scr[seq] = nrun

    for ahead in range(SB_PREFETCH_TRIPS):
        start_pages(ahead * per_trip)

    def body(u, carry):
        wait_pages(u * per_trip)
        start_pages((u + SB_PREFETCH_TRIPS) * per_trip)
        v0 = SB_VISITS_PER_TRIP * u
        zs = page_logits(u)
        stage(v0, 0, 1)
        parts = page_suffix(zs)
        stage(v0 + 1, 1, 0)
        page_finish(u, parts)
        for i in range(2, SB_VISITS_PER_TRIP):
            stage(v0 + i, i % 2, (i + 1) % 2)
        return carry

    lax.fori_loop(0, trips, body, 0)
    for ahead in range(SB_PREFETCH_TRIPS):
        wait_pages((trips + ahead) * per_trip)

    for s in range(seqs_per_step):
        picked = jnp.where(own_head, sacc_scr[s], 0.0)
        for t in range(n_tok):
            so_ref[s * n_tok + t:s * n_tok + t + 1, :] = jnp.sum(
                picked[t * SB_HEADS:(t + 1) * SB_HEADS, :], axis=0, keepdims=True)


def _sb_attention(proj, proj_meta, proj_s, cache_k, cache_v, page_table, sb_bias, *, n_seq, seq_len, tile,
                  n_dec, n_tok):
    p3 = proj.reshape(n_seq, seq_len, PROJ_W)
    slab = lambda col: col // LANES
    n_main = seq_len // tile
    pair_rows = HEADS_PER_SLAB * tile
    n_pairs = SB_HEADS // HEADS_PER_SLAB
    width = SB_HEADS * SB_HEAD_DIM
    n_pages = page_table.shape[1]
    page_size = cache_k.shape[1]
    seqs_per_step = n_dec // (n_seq * n_pairs)
    assert seqs_per_step * n_seq * n_pairs == n_dec and n_pages % SB_PAGES_PER_TRIP == 0
    step_rows = seqs_per_step * n_tok
    n_rows = n_tok * SB_HEADS
    ck = cache_k.transpose(0, 2, 3, 1).reshape(cache_k.shape[0], width, page_size)
    cv = cache_v.transpose(0, 2, 3, 1).reshape(cache_v.shape[0], width, page_size)
    bias_rows = jnp.tile(sb_bias, n_tok).reshape(n_rows, 1)
    ring = (SB_PREFETCH_TRIPS + 1) * SB_PAGES_PER_TRIP
    visits = [(q, t) for q in range(n_main) for t in range(q + 2)]
    n_visits = len(visits)
    visits += [(n_main - 1, n_main + 1)] * (SB_VISITS_PER_TRIP + 2)
    vq = jnp.asarray([q for q, _ in visits], jnp.int32)
    vt = jnp.asarray([t for _, t in visits], jnp.int32)
    assert seqs_per_step * n_pages <= pl.cdiv(n_visits + 1, SB_VISITS_PER_TRIP) * SB_PAGES_PER_TRIP
    seq_blk = lambda col: pl.BlockSpec((None, seq_len, LANES), lambda b, p, *_: (b, 0, slab(col) + p))
    meta_blk = lambda col: pl.BlockSpec((LANES, LANES), lambda b, p, *_: (0, slab(col) + p))
    new_blk = lambda col: pl.BlockSpec((step_rows, width), lambda b, p, *_: (b * n_pairs + p, col // width))
    grid_spec = pltpu.PrefetchScalarGridSpec(
        num_scalar_prefetch=3,
        grid=(n_seq, n_pairs),
        in_specs=[pl.BlockSpec(memory_space=pltpu.SMEM),
                  seq_blk(COL_QB), seq_blk(COL_KB), seq_blk(COL_VB), meta_blk(COL_KB), meta_blk(COL_VB),
                  pl.BlockSpec((n_rows, 1), lambda b, p, *_: (0, 0)),
                  new_blk(COL_QB), new_blk(COL_KB), new_blk(COL_VB),
                  pl.BlockSpec(memory_space=pl.ANY), pl.BlockSpec(memory_space=pl.ANY)],
        out_specs=[pl.BlockSpec((None, seq_len, LANES), lambda b, p, *_: (b, 0, p)),
                   pl.BlockSpec((step_rows, width), lambda b, p, *_: (b * n_pairs + p, 0))],
        scratch_shapes=[pltpu.VMEM((n_main, pair_rows, LANES), BF16),
                        pltpu.VMEM((n_main + 1, tile, LANES), BF16),
                        pltpu.VMEM((n_main + 1, pair_rows, LANES), BF16),
                        pltpu.VMEM((4, pair_rows, tile), F32),
                        pltpu.VMEM((tile, tile), BF16),
                        pltpu.VMEM((2, pair_rows, tile), F32),
                        pltpu.VMEM((2, pair_rows, tile), F32),
                        pltpu.VMEM((2, pair_rows, tile), F32),
                        pltpu.VMEM((2, pair_rows, LANES), F32),
                        pltpu.VMEM((pair_rows, LANES), F32),
                        pltpu.VMEM((tile, LANES), F32),
                        pltpu.VMEM((seqs_per_step, n_rows, width), BF16),
                        pltpu.VMEM((seqs_per_step, n_rows, width), F32),
                        pltpu.VMEM((seqs_per_step, n_rows, 1), F32),
                        pltpu.VMEM((page_size, page_size), BF16),
                        pltpu.VMEM((ring, width, page_size), F32),
                        pltpu.VMEM((ring, width, page_size), F32),
                        pltpu.SemaphoreType.DMA((2, ring))],
    )
    out_p, out_s = pl.pallas_call(
        functools.partial(_sb_kernel, tile=tile, n_main=n_main, n_visits=n_visits, n_tok=n_tok,
                          seqs_per_step=seqs_per_step, n_pages=n_pages),
        grid_spec=grid_spec,
        out_shape=[jax.ShapeDtypeStruct((n_seq, seq_len, width), BF16),
                   jax.ShapeDtypeStruct((n_dec * n_tok, width), F32)],
        compiler_params=pltpu.CompilerParams(
            dimension_semantics=("arbitrary", "arbitrary"), vmem_limit_bytes=VMEM_LIMIT),
    )(vq, vt, page_table, sb_bias, p3, p3, p3, proj_meta, proj_meta, bias_rows, proj_s, proj_s, proj_s, ck, cv)
    return out_p.reshape(n_seq * seq_len, width), out_s


def _merge_kernel(oa_ref, ob_ref, ga_ref, gb_ref, h_ref, woa_ref, wob_ref, wout_ref, g_ref, out_ref):
    pa = _dot(oa_ref[...].astype(BF16), woa_ref[...])
    pb = _dot(ob_ref[...].astype(BF16), wob_ref[...])
    mixed = jax.nn.sigmoid(ga_ref[...]) * pa + jax.nn.sigmoid(gb_ref[...]) * pb
    out_ref[...] = h_ref[...] + _rms(_dot(mixed.astype(BF16), wout_ref[...]), g_ref[...])


def _merge(o_a, o_b, proj, h, w_o_gla, w_o_sb, w_out, g_post, *, tm):
    m, d = h.shape
    row = pl.BlockSpec((tm, d), lambda i: (i, 0))
    return pl.pallas_call(
        _merge_kernel,
        grid=(m // tm,),
        in_specs=[row, row,
                  pl.BlockSpec((tm, d), lambda i: (i, COL_GA // d)),
                  pl.BlockSpec((tm, d), lambda i: (i, COL_GB // d)),
                  row, _const_spec(w_o_gla.shape), _const_spec(w_o_sb.shape), _const_spec(w_out.shape),
                  _const_spec((1, d))],
        out_specs=row,
        out_shape=jax.ShapeDtypeStruct((m, d), F32),
        compiler_params=pltpu.CompilerParams(
            dimension_semantics=("arbitrary",), vmem_limit_bytes=VMEM_LIMIT),
    )(o_a, o_b, proj, proj, h, w_o_gla, w_o_sb, w_out, g_post)


def _row_tile(m, cap):
    return m if m <= cap else cap


def kernel(x_prompt, x_sample, cache_k, cache_v, state_gla, page_table, meta_tokens,
           ffn1_pre_g, ffn1_w_gate, ffn1_w_up, ffn1_w_down, ffn1_post_g,
           mix_pre_g, w_in, w_gk2, b_gk, gla_norm_g, sb_bias, w_o_gla, w_o_sb, w_out, mix_post_g,
           ffn2_pre_g, ffn2_w_gate, ffn2_w_up, ffn2_w_down, ffn2_post_g):
    n_seq, seq_len, d = x_prompt.shape
    n_dec, n_tok, _ = x_sample.shape
    assert ffn1_pre_g.shape[0] == 1, "single layer"
    qk = GLA_HEADS * GLA_DK
    v_w = GLA_HEADS * GLA_DV
    rank = w_gk2.shape[1]

    bf = lambda w: w[0].astype(BF16)
    w1g, w1u, w1d = bf(ffn1_w_gate), bf(ffn1_w_up), bf(ffn1_w_down)
    w2g, w2u, w2d = bf(ffn2_w_gate), bf(ffn2_w_up), bf(ffn2_w_down)
    w_in0 = w_in[0]
    gk0 = 2 * qk + v_w
    w_main = jnp.concatenate([w_in0[:, :gk0], w_in0[:, gk0 + rank:]], axis=1).astype(BF16)
    w_gk1 = jnp.pad(w_in0[:, gk0:gk0 + rank], ((0, 0), (0, LANES - rank))).astype(BF16)
    w_gk2p = jnp.pad(w_gk2[0], ((0, LANES - rank), (0, 0))).astype(BF16)
    woa, wob, wout = bf(w_o_gla), bf(w_o_sb), bf(w_out)
    row = lambda g: g[0].reshape(1, -1)
    bias = sb_bias[0]

    def pre_mix(x, tm):
        h, u = _ffn(x, row(ffn1_pre_g), w1g, w1u, w1d, row(ffn1_post_g), row(mix_pre_g), tm=tm)
        proj, glog = _proj(u, w_main, w_gk1, w_gk2p, row(b_gk), tm=_row_tile(x.shape[0], 2048), tn=1024)
        return h, u, proj, glog

    def post_mix(h, proj, o_a, o_b, tm):
        h2 = _merge(o_a, o_b, proj, h, woa, wob, wout, row(mix_post_g), tm=tm)
        return _ffn(h2, row(ffn2_pre_g), w2g, w2u, w2d, row(ffn2_post_g), tm=tm)

    x_meta = jnp.pad(meta_tokens.astype(F32), ((0, LANES - N_META), (0, 0)))
    _, u_m, proj_m, glog_m = pre_mix(x_meta, LANES)

    m_p = n_seq * seq_len
    tm_p = _row_tile(m_p, 512)
    h_p, u_p, proj_p, glog_p = pre_mix(x_prompt.reshape(m_p, d), tm_p)
    oa_p, s_p = _gla(proj_p, glog_p, row(gla_norm_g), n_seq=n_seq, seq_len=seq_len,
                     rows_per_step=_row_tile(seq_len, 512), chunk=GLA_CHUNK, n_heads=2,
                     meta=(proj_m, glog_m))

    m_s = n_dec * n_tok
    tm_s = _row_tile(m_s, 512)
    h_s, _, proj_s, glog_s = pre_mix(x_sample.reshape(m_s, d), tm_s)

    ob_p, ob_s = _sb_attention(proj_p, proj_m, proj_s, cache_k[0], cache_v[0], page_table, bias,
                               n_seq=n_seq, seq_len=seq_len, tile=MXU_DIM, n_dec=n_dec, n_tok=n_tok)
    y_p = post_mix(h_p, proj_p, oa_p, ob_p, tm_p)

    pad_tok = lambda a: jnp.pad(a.reshape(n_dec, n_tok, -1), ((0, 0), (0, SUBLANES - n_tok), (0, 0))
                                ).reshape(n_dec * SUBLANES, -1)
    oa_s, s_s = _gla(pad_tok(proj_s[:, :COL_QB]), pad_tok(glog_s), row(gla_norm_g), n_seq=n_dec,
                     seq_len=SUBLANES, rows_per_step=SUBLANES, chunk=SUBLANES, n_heads=GLA_HEADS,
                     state0=state_gla[0])
    oa_s = oa_s.reshape(n_dec, SUBLANES, v_w)[:, :n_tok].reshape(m_s, v_w)
    y_s = post_mix(h_s, proj_s, oa_s, ob_s, tm_s)

    heads = (SB_HEADS, SB_HEAD_DIM)
    w_sb = SB_HEADS * SB_HEAD_DIM

    u_full = jnp.concatenate([jnp.broadcast_to(u_m[:N_META][None], (n_seq, N_META, d)),
                              u_p.reshape(n_seq, seq_len, d)], axis=1)
    k_t, v_t = _kv_t(w_main[:, COL_KB:COL_KB + w_sb].T, w_main[:, COL_VB:COL_VB + w_sb].T, u_full,
                     tp=512)
    kv_out = lambda a: a.reshape(n_seq, *heads, N_META + seq_len).transpose(0, 3, 1, 2)[None]

    return (y_p.reshape(n_seq, seq_len, d),
            y_s.reshape(n_dec, n_tok, d),
            kv_out(k_t), kv_out(v_t),
            s_p[None],
            proj_s[:, COL_KB:COL_KB + w_sb].reshape(1, n_dec, n_tok, *heads),
            proj_s[:, COL_VB:COL_VB + w_sb].reshape(1, n_dec, n_tok, *heads),
            s_s[None])
```

```python
import functools

import jax
import jax.numpy as jnp
from jax import lax
from jax.experimental import pallas as pl
from jax.experimental.pallas import tpu as pltpu

F32 = jnp.float32
BF16 = jnp.bfloat16

EPS = 1e-6
N_META = 16
GLA_HEADS = 4
GLA_DK = 128
GLA_DV = 256
GLA_TAU = 16.0
GLA_CHUNK = 64
SB_HEADS = 16
SB_HEAD_DIM = 64
LANES = 128
SUBLANES = 8
MXU_DIM = 256
HEADS_PER_SLAB = LANES // SB_HEAD_DIM
VMEM_LIMIT = 56 * 1024 * 1024
NEG_BIG = -1e30
LOG2E = 1.4426950408889634
SB_VISITS_PER_TRIP = 4
SB_PAGES_PER_TRIP = 2
SB_PAGE_GROUP = 2
SB_PREFETCH_TRIPS = 4

COL_QA, COL_KA, COL_VA, COL_RA = 0, 512, 1024, 2048
COL_QB, COL_KB, COL_VB, COL_GA, COL_GB = 3072, 4096, 5120, 6144, 7168
PROJ_W = 8192


def _rms(x, g):
    return x * lax.rsqrt(jnp.mean(x * x, axis=-1, keepdims=True) + EPS) * g


def _const_spec(shape):
    return pl.BlockSpec(shape, lambda *_: (0,) * len(shape), pipeline_mode=pl.Buffered(1))


def _dot(a, b):
    return jnp.dot(a, b, preferred_element_type=F32)


def _dot_nt(a, b):
    return lax.dot_general(a, b, (((1,), (1,)), ((), ())), preferred_element_type=F32)


def _dot_tn(a, b):
    return lax.dot_general(a, b, (((0,), (0,)), ((), ())), preferred_element_type=F32)


def _split3(x):
    hi = x.astype(BF16)
    r1 = x - hi.astype(F32)
    mid = r1.astype(BF16)
    lo = (r1 - mid.astype(F32)).astype(BF16)
    return hi, mid, lo


def _ffn_kernel(x_ref, gpre_ref, wg_ref, wu_ref, wd_ref, gpost_ref, *rest, ff_chunk, emit_u):
    if emit_u:
        gnext_ref, h_ref, u_ref, act_ref = rest
    else:
        h_ref, act_ref = rest
    x = x_ref[...]
    u = _rms(x, gpre_ref[...]).astype(BF16)
    d_ff = wg_ref.shape[1]
    for c in range(d_ff // ff_chunk):
        sl = slice(c * ff_chunk, (c + 1) * ff_chunk)
        gate = _dot(u, wg_ref[:, sl])
        up = _dot(u, wu_ref[:, sl])
        act_ref[:, sl] = (gate * jax.nn.sigmoid(gate) * up).astype(BF16)
    y = _dot(act_ref[...], wd_ref[...])
    h = x + 0.5 * _rms(y, gpost_ref[...])
    h_ref[...] = h
    if emit_u:
        u_ref[...] = _rms(h, gnext_ref[...]).astype(BF16)


def _ffn(x, g_pre, wg, wu, wd, g_post, g_next=None, *, tm):
    m, d = x.shape
    d_ff = wg.shape[1]
    emit_u = g_next is not None
    row = pl.BlockSpec((tm, d), lambda i: (i, 0))
    in_specs = [row, _const_spec((1, d)), _const_spec((d, d_ff)), _const_spec((d, d_ff)),
                _const_spec((d_ff, d)), _const_spec((1, d))]
    args = [x, g_pre, wg, wu, wd, g_post]
    out_shape = [jax.ShapeDtypeStruct((m, d), F32)]
    out_specs = [row]
    if emit_u:
        in_specs.append(_const_spec((1, d)))
        args.append(g_next)
        out_shape.append(jax.ShapeDtypeStruct((m, d), BF16))
        out_specs.append(row)
    outs = pl.pallas_call(
        functools.partial(_ffn_kernel, ff_chunk=MXU_DIM, emit_u=emit_u),
        grid=(m // tm,),
        in_specs=in_specs,
        out_specs=out_specs,
        out_shape=out_shape,
        scratch_shapes=[pltpu.VMEM((tm, d_ff), BF16)],
        compiler_params=pltpu.CompilerParams(
            dimension_semantics=("arbitrary",), vmem_limit_bytes=VMEM_LIMIT),
    )(*args)
    return outs if emit_u else outs[0]


def _proj_kernel(u_ref, w_ref, wgk1_ref, wgk2_ref, bgk_ref, p_ref, g_ref):
    u = u_ref[...]
    p_ref[...] = _dot(u, w_ref[...])

    @pl.when(pl.program_id(1) == 0)
    def _():
        low = _dot(u, wgk1_ref[...]).astype(BF16)
        zg = _dot(low, wgk2_ref[...]) + bgk_ref[...]
        g_ref[...] = (jnp.minimum(zg, 0.0) - jnp.log(1.0 + jnp.exp(-jnp.abs(zg)))) * (1.0 / GLA_TAU)


def _proj(u, w_main, w_gk1, w_gk2, b_gk, *, tm, tn):
    m, d = u.shape
    n = w_main.shape[1]
    qk = w_gk2.shape[1]
    return pl.pallas_call(
        _proj_kernel,
        grid=(m // tm, n // tn),
        in_specs=[pl.BlockSpec((tm, d), lambda i, j: (i, 0)),
                  pl.BlockSpec((d, tn), lambda i, j: (0, j)),
                  _const_spec(w_gk1.shape), _const_spec(w_gk2.shape), _const_spec((1, qk))],
        out_specs=[pl.BlockSpec((tm, tn), lambda i, j: (i, j)),
                   pl.BlockSpec((tm, qk), lambda i, j: (i, 0))],
        out_shape=[jax.ShapeDtypeStruct((m, n), F32), jax.ShapeDtypeStruct((m, qk), F32)],
        compiler_params=pltpu.CompilerParams(
            dimension_semantics=("arbitrary", "arbitrary"), vmem_limit_bytes=VMEM_LIMIT),
    )(u, w_main, w_gk1, w_gk2, b_gk)


def _kv_t_kernel(wk_ref, wv_ref, u_ref, k_ref, v_ref):
    u = u_ref[...]
    k_ref[...] = _dot_nt(wk_ref[...], u)
    v_ref[...] = _dot_nt(wv_ref[...], u)


def _kv_t(wk_t, wv_t, u_full, *, tp):
    width, d = wk_t.shape
    n_seq, n_pos, _ = u_full.shape
    out = pl.BlockSpec((None, width, tp), lambda b, j: (b, 0, j))
    return pl.pallas_call(
        _kv_t_kernel,
        grid=(n_seq, pl.cdiv(n_pos, tp)),
        in_specs=[_const_spec((width, d)), _const_spec((width, d)),
                  pl.BlockSpec((None, tp, d), lambda b, j: (b, j, 0))],
        out_specs=[out, out],
        out_shape=[jax.ShapeDtypeStruct((n_seq, width, n_pos), F32)] * 2,
        compiler_params=pltpu.CompilerParams(
            dimension_semantics=("arbitrary", "arbitrary"), vmem_limit_bytes=VMEM_LIMIT),
    )(wk_t, wv_t, u_full)


def _gla_local(problems, c):
    rows = lax.broadcasted_iota(jnp.int32, (c, c), 0)
    cols = lax.broadcasted_iota(jnp.int32, (c, c), 1)
    causal = cols <= rows
    tri = causal.astype(BF16)
    ones_c = jnp.ones((c, LANES), BF16)
    parts = [_split3(g) for _, _, _, g in problems]
    cums = [_dot(tri, p[0]) + _dot(tri, p[1]) + _dot(tri, p[2]) for p in parts]
    sums = [_dot_tn(p[0], ones_c) + _dot_tn(p[1], ones_c) + _dot_tn(p[2], ones_c) for p in parts]
    q_ins, k_ins, k_ends, vbs = [], [], [], []
    for (q, k, v, _), b in zip(problems, cums):
        q_ins.append((q * (GLA_DK ** -0.5) * jnp.exp(b)).astype(BF16))
        k_ins.append((k * jnp.exp(-b)).astype(BF16))
        k_ends.append((k * jnp.exp(b[c - 1:c, :] - b)).astype(BF16))
        vbs.append(v.astype(BF16))
    scores = [_dot_nt(qi, ki) for qi, ki in zip(q_ins, k_ins)]
    o_intra = [_dot(jnp.where(causal, a, 0.0).astype(BF16), vb) for a, vb in zip(scores, vbs)]
    d_state = [_dot_tn(ke, vb) for ke, vb in zip(k_ends, vbs)]
    decays = [jnp.concatenate([jnp.exp(bs)] * (GLA_DV // LANES), axis=1) for bs in sums]
    return list(zip(q_ins, o_intra, d_state, decays))


def _gla_kernel(*refs, n_seqs, n_heads, chunk, n_chunks, lead):
    if lead:
        km_ref, vm_ref, gm_ref = refs[:3]
        refs = refs[3:]
    else:
        s0_ref = refs[0]
        refs = refs[1:]
    q_ref, k_ref, v_ref, g_ref, r_ref, gn_ref, o_ref, sfin_ref, s_scr = refs
    step = pl.program_id(2)
    dk = lambda h: slice(h * GLA_DK, (h + 1) * GLA_DK)
    dv = lambda h: slice(h * GLA_DV, (h + 1) * GLA_DV)
    rows = lambda sq, ci: slice((sq * n_chunks + ci) * chunk, (sq * n_chunks + ci + 1) * chunk)
    chains = [(sq, h) for sq in range(n_seqs) for h in range(n_heads)]

    @pl.when(step == 0)
    def _():
        if lead:
            zeros = jnp.zeros((lead, GLA_DK), F32)
            metas = _gla_local([(zeros, km_ref[:, dk(h)], vm_ref[:, dv(h)], gm_ref[:, dk(h)])
                                for h in range(n_heads)], lead)
            for h in range(n_heads):
                s_scr[0, h] = metas[h][2]
        else:
            s_scr[...] = s0_ref[...]

    problems = []
    for sq, h in chains:
        for ci in range(n_chunks):
            sl = rows(sq, ci)
            problems.append((q_ref[sl, dk(h)], k_ref[sl, dk(h)], v_ref[sl, dv(h)], g_ref[sl, dk(h)]))
    local = _gla_local(problems, chunk)
    gn = gn_ref[...]
    for c, (sq, h) in enumerate(chains):
        s = s_scr[sq, h]
        for ci in range(n_chunks):
            sl = rows(sq, ci)
            q_in, o_intra, d_state, decay = local[c * n_chunks + ci]
            o = o_intra + _dot(q_in, s.astype(BF16))
            s = decay * s + d_state
            r = r_ref[sl, dv(h)]
            o_ref[sl, dv(h)] = (_rms(o, gn) * (r * jax.nn.sigmoid(r))).astype(BF16)
        s_scr[sq, h] = s

    @pl.when(step == pl.num_programs(2) - 1)
    def _():
        sfin_ref[...] = s_scr[...]


def _gla(proj, glog, gla_norm_g, *, n_seq, seq_len, rows_per_step, chunk, n_heads, n_seqs=1, meta=None,
         state0=None):
    steps = seq_len // rows_per_step
    n_chunks = rows_per_step // chunk
    groups = GLA_HEADS // n_heads
    wk, wv = n_heads * GLA_DK, n_heads * GLA_DV
    assert n_seqs == 1 or (steps == 1 and meta is None)
    assert n_seq % n_seqs == 0

    def rows(width, col0):
        return pl.BlockSpec((n_seqs * rows_per_step, width),
                            lambda b, h, s: (b * steps + s, col0 // width + h))

    state_spec = pl.BlockSpec((n_seqs, n_heads, GLA_DK, GLA_DV), lambda b, h, s: (b, h, 0, 0))
    in_specs, args = [], []
    if meta is not None:
        pm, gm = meta
        in_specs += [pl.BlockSpec((N_META, wk), lambda b, h, s: (0, COL_KA // wk + h)),
                     pl.BlockSpec((N_META, wv), lambda b, h, s: (0, COL_VA // wv + h)),
                     pl.BlockSpec((N_META, wk), lambda b, h, s: (0, h))]
        args += [pm, pm, gm]
    else:
        in_specs.append(state_spec)
        args.append(state0)
    in_specs += [rows(wk, COL_QA), rows(wk, COL_KA), rows(wv, COL_VA), rows(wk, 0), rows(wv, COL_RA),
                 _const_spec((1, GLA_DV))]
    args += [proj, proj, proj, glog, proj, gla_norm_g]
    m = n_seq * seq_len
    return pl.pallas_call(
        functools.partial(_gla_kernel, n_seqs=n_seqs, n_heads=n_heads, chunk=chunk, n_chunks=n_chunks,
                          lead=N_META if meta is not None else 0),
        grid=(n_seq // n_seqs, groups, steps),
        in_specs=in_specs,
        out_specs=[rows(wv, 0), state_spec],
        out_shape=[jax.ShapeDtypeStruct((m, GLA_HEADS * GLA_DV), BF16),
                   jax.ShapeDtypeStruct((n_seq, GLA_HEADS, GLA_DK, GLA_DV), F32)],
        scratch_shapes=[pltpu.VMEM((n_seqs, n_heads, GLA_DK, GLA_DV), F32)],
        compiler_params=pltpu.CompilerParams(
            dimension_semantics=("arbitrary", "arbitrary", "arbitrary"), vmem_limit_bytes=VMEM_LIMIT),
    )(*args)


def _sb_logs(zb):
    nl = jnp.maximum(zb, 0.0) + jnp.log(1.0 + jnp.exp2(jnp.abs(zb) * (-LOG2E)))
    return zb - nl, nl


def _suffix_matrix(t):
    rows = lax.broadcasted_iota(jnp.int32, (t, t), 0)
    cols = lax.broadcasted_iota(jnp.int32, (t, t), 1)
    return (rows > cols).astype(BF16)


def _sb_suffix(nl, suffix):
    later = _dot(nl.astype(BF16), suffix)
    return later, jnp.sum(nl, axis=1, keepdims=True)


def _own_head_mask(n_rows):
    width = SB_HEADS * SB_HEAD_DIM
    row_id = lax.broadcasted_iota(jnp.int32, (n_rows, width), 0)
    col_id = lax.broadcasted_iota(jnp.int32, (n_rows, width), 1)
    return lax.shift_right_logical(col_id, SB_HEAD_DIM.bit_length() - 1) == (row_id & (SB_HEADS - 1))


def _sb_kernel(vq_ref, vt_ref, pt_ref, bias_ref, q_ref, k_ref, v_ref, km_ref, vm_ref,
               sbias_ref, sq_ref, skn_ref, svn_ref, ck_hbm, cv_hbm,
               o_ref, so_ref,
               q_scr, k_scr, vv_scr, bt_scr, sfx_scr, z_scr, w_scr, later_scr, tot_scr, nrun_scr, acc_scr,
               sqbd_scr, sacc_scr, srun_scr, spage_sfx_scr, kbuf, vbuf, sems,
               *, tile, n_main, n_visits, n_tok, seqs_per_step, n_pages):
    pair = pl.program_id(1)
    low_lanes = lax.broadcasted_iota(jnp.int32, (1, LANES), 1) < SB_HEAD_DIM
    both = lambda x: jnp.concatenate([jnp.where(low_lanes, x, 0.0), jnp.where(low_lanes, 0.0, x)], axis=0)
    DIAG, FULL, META, NONE = range(4)

    for j in range(n_main):
        sl = slice(j * tile, (j + 1) * tile)
        q_scr[j] = both(q_ref[sl, :] * (SB_HEAD_DIM ** -0.5)).astype(BF16)
        k_scr[j] = k_ref[sl, :].astype(BF16)
        vv_scr[j] = both(v_ref[sl, :]).astype(BF16)
    pad = jnp.zeros((tile - LANES, LANES), F32)
    k_scr[n_main] = jnp.concatenate([km_ref[...], pad], axis=0).astype(BF16)
    vv_scr[n_main] = both(jnp.concatenate([vm_ref[...], pad], axis=0)).astype(BF16)
    rows = lax.broadcasted_iota(jnp.int32, (tile, tile), 0)
    cols = lax.broadcasted_iota(jnp.int32, (tile, tile), 1)
    kinds = {DIAG: jnp.where(cols < rows, 0.0, NEG_BIG), FULL: jnp.zeros((tile, tile), F32),
             META: jnp.where(cols < N_META, 0.0, NEG_BIG), NONE: jnp.full((tile, tile), NEG_BIG, F32)}
    for kind, mask in kinds.items():
        for h in range(HEADS_PER_SLAB):
            bt_scr[kind, h * tile:(h + 1) * tile, :] = mask + bias_ref[HEADS_PER_SLAB * pair + h]
    sfx_scr[...] = _suffix_matrix(tile)

    def key_slot(q, t):
        return jnp.where(t <= q, q - t, n_main)

    def kind_of(q, t):
        return jnp.where(t == 0, DIAG, jnp.where(t <= q, FULL, jnp.where(t == q + 1, META, NONE)))

    def logits(v):
        q, t = vq_ref[v], vt_ref[v]
        return _dot_nt(q_scr[q], k_scr[key_slot(q, t)])

    def stage(v, cur, nxt):
        z_scr[nxt] = logits(v + 1)
        w, nl = _sb_logs(z_scr[cur] + bt_scr[kind_of(vq_ref[v], vt_ref[v])])
        later, tot = _sb_suffix(nl, sfx_scr[...])
        prev = jnp.maximum(v - 1, 0)
        q, t = vq_ref[prev], vt_ref[prev]
        fresh = t == 0
        nrun = jnp.where(fresh, 0.0, nrun_scr[...])
        a = jnp.exp(w_scr[nxt] - later_scr[nxt] - jnp.concatenate([nrun, nrun], axis=1)).astype(BF16)
        acc = jnp.where(fresh, 0.0, acc_scr[...]) + _dot(jnp.concatenate([a[:tile], a[tile:]], axis=1),
                                                          vv_scr[key_slot(q, t)])
        acc_scr[...] = acc
        o_ref[pl.ds(pl.multiple_of(q * tile, tile), tile), :] = acc.astype(BF16)
        nrun_scr[...] = nrun + tot_scr[nxt]
        w_scr[cur] = w
        later_scr[cur] = later
        tot_scr[cur] = jnp.broadcast_to(tot, tot_scr.shape[1:])

    z_scr[0] = logits(0)
    w_scr[1] = jnp.full(w_scr.shape[1:], NEG_BIG, F32)
    later_scr[1] = jnp.zeros(later_scr.shape[1:], F32)
    tot_scr[1] = jnp.zeros(tot_scr.shape[1:], F32)
    nrun_scr[...] = jnp.zeros(nrun_scr.shape, F32)
    acc_scr[...] = jnp.zeros(acc_scr.shape, F32)

    n_rows = n_tok * SB_HEADS
    width = SB_HEADS * SB_HEAD_DIM
    own_head = _own_head_mask(n_rows)
    sbias = sbias_ref[...]
    tok_of_row = lax.shift_right_logical(lax.broadcasted_iota(jnp.int32, (n_rows, 1), 0),
                                         SB_HEADS.bit_length() - 1)
    for s in range(seqs_per_step):
        r0 = s * n_tok
        q_rows = jnp.concatenate(
            [jnp.broadcast_to(sq_ref[r0 + t:r0 + t + 1, :] * (SB_HEAD_DIM ** -0.5), (SB_HEADS, width))
             for t in range(n_tok)], axis=0)
        qbd = jnp.where(own_head, q_rows, 0.0).astype(BF16)
        sqbd_scr[s] = qbd
        qf = qbd.astype(F32)
        nrun = jnp.zeros((n_rows, 1), F32)
        acc = jnp.zeros((n_rows, width), F32)
        for j in reversed(range(n_tok)):
            kj = skn_ref[r0 + j:r0 + j + 1, :].astype(BF16).astype(F32)
            vj = svn_ref[r0 + j:r0 + j + 1, :].astype(BF16).astype(F32)
            z = jnp.sum(qf * kj, axis=-1, keepdims=True) + sbias
            w, nl = _sb_logs(jnp.where(tok_of_row > j, z, NEG_BIG))
            acc = acc + jnp.exp(w - nrun) * vj
            nrun = nrun + nl
        sacc_scr[s] = acc
        srun_scr[s] = nrun
    spage_sfx_scr[...] = _suffix_matrix(kbuf.shape[2])

    trips = pl.cdiv(n_visits + 1, SB_VISITS_PER_TRIP)
    per_trip = SB_PAGES_PER_TRIP
    n_total = seqs_per_step * n_pages
    ring = kbuf.shape[0]
    seq0 = (pl.program_id(0) * pl.num_programs(1) + pair) * seqs_per_step

    def page_copies(g):
        g = jnp.asarray(g, jnp.int32)
        gc = jnp.minimum(g, n_total - 1)
        pool = pt_ref[seq0 + lax.div(gc, jnp.int32(n_pages)), n_pages - 1 - lax.rem(gc, jnp.int32(n_pages))]
        slot = lax.rem(g, jnp.int32(ring))
        return (pltpu.make_async_copy(ck_hbm.at[pool], kbuf.at[slot], sems.at[0, slot]),
                pltpu.make_async_copy(cv_hbm.at[pool], vbuf.at[slot], sems.at[1, slot]))

    def start_pages(g0):
        for i in range(per_trip):
            for copy in page_copies(g0 + i):
                copy.start()

    def wait_pages(g0):
        for i in range(per_trip):
            for copy in page_copies(g0 + i):
                copy.wait()

    def page_logits(g0):
        seq = lax.div(jnp.minimum(g0, n_total - 1), jnp.int32(n_pages))
        off = sbias + jnp.where(g0 < n_total, 0.0, NEG_BIG)
        return [_dot(sqbd_scr[seq], kbuf[lax.rem(g0 + i, jnp.int32(ring))].astype(BF16)) + off
                for i in range(SB_PAGE_GROUP)]

    def page_suffix(zs):
        logs = [_sb_logs(z) for z in zs]
        sums = [_sb_suffix(nl, spage_sfx_scr[...]) for _, nl in logs]
        return [(w, later, tot) for (w, _), (later, tot) in zip(logs, sums)]

    def page_finish(g0, parts):
        seq = lax.div(jnp.minimum(g0, n_total - 1), jnp.int32(n_pages))
        nrun = srun_scr[seq]
        acc = sacc_scr[seq]
        for i, (w, later, tot) in enumerate(parts):
            a = jnp.exp(w - later - nrun).astype(BF16)
            acc = acc + _dot_nt(a, vbuf[lax.rem(g0 + i, jnp.int32(ring))].astype(BF16))
            nrun = nrun + tot
        sacc_scr[seq] = acc
        srun_scr[seq] = nrun

    for ahead in range(SB_PREFETCH_TRIPS):
        start_pages(ahead * per_trip)

    groups = per_trip // SB_PAGE_GROUP
    visits_per_group = SB_VISITS_PER_TRIP // groups

    def body(u, carry):
        wait_pages(u * per_trip)
        start_pages((u + SB_PREFETCH_TRIPS) * per_trip)
        for k in range(groups):
            v0 = SB_VISITS_PER_TRIP * u + k * visits_per_group
            g0 = u * per_trip + k * SB_PAGE_GROUP
            zs = page_logits(g0)
            stage(v0, 0, 1)
            parts = page_suffix(zs)
            stage(v0 + 1, 1, 0)
            page_finish(g0, parts)
            for i in range(2, visits_per_group):
                stage(v0 + i, i % 2, (i + 1) % 2)
        return carry

    lax.fori_loop(0, trips, body, 0)
    for ahead in range(SB_PREFETCH_TRIPS):
        wait_pages((trips + ahead) * per_trip)

    for s in range(seqs_per_step):
        picked = jnp.where(own_head, sacc_scr[s], 0.0)
        for t in range(n_tok):
            so_ref[s * n_tok + t:s * n_tok + t + 1, :] = jnp.sum(
                picked[t * SB_HEADS:(t + 1) * SB_HEADS, :], axis=0, keepdims=True)


def _sb_attention(proj, proj_meta, proj_s, cache_k, cache_v, page_table, sb_bias, *, n_seq, seq_len, tile,
                  n_dec, n_tok):
    p3 = proj.reshape(n_seq, seq_len, PROJ_W)
    slab = lambda col: col // LANES
    n_main = seq_len // tile
    pair_rows = HEADS_PER_SLAB * tile
    n_pairs = SB_HEADS // HEADS_PER_SLAB
    width = SB_HEADS * SB_HEAD_DIM
    n_pages = page_table.shape[1]
    page_size = cache_k.shape[1]
    seqs_per_step = n_dec // (n_seq * n_pairs)
    assert seqs_per_step * n_seq * n_pairs == n_dec and n_pages % SB_PAGES_PER_TRIP == 0
    step_rows = seqs_per_step * n_tok
    n_rows = n_tok * SB_HEADS
    ck = cache_k.transpose(0, 2, 3, 1).reshape(cache_k.shape[0], width, page_size)
    cv = cache_v.transpose(0, 2, 3, 1).reshape(cache_v.shape[0], width, page_size)
    bias_rows = jnp.tile(sb_bias, n_tok).reshape(n_rows, 1)
    ring = (SB_PREFETCH_TRIPS + 1) * SB_PAGES_PER_TRIP
    visits = [(q, t) for q in range(n_main) for t in range(q + 2)]
    n_visits = len(visits)
    visits += [(n_main - 1, n_main + 1)] * (SB_VISITS_PER_TRIP + 2)
    vq = jnp.asarray([q for q, _ in visits], jnp.int32)
    vt = jnp.asarray([t for _, t in visits], jnp.int32)
    assert seqs_per_step * n_pages <= pl.cdiv(n_visits + 1, SB_VISITS_PER_TRIP) * SB_PAGES_PER_TRIP
    seq_blk = lambda col: pl.BlockSpec((None, seq_len, LANES), lambda b, p, *_: (b, 0, slab(col) + p))
    meta_blk = lambda col: pl.BlockSpec((LANES, LANES), lambda b, p, *_: (0, slab(col) + p))
    new_blk = lambda col: pl.BlockSpec((step_rows, width), lambda b, p, *_: (b * n_pairs + p, col // width))
    grid_spec = pltpu.PrefetchScalarGridSpec(
        num_scalar_prefetch=3,
        grid=(n_seq, n_pairs),
        in_specs=[pl.BlockSpec(memory_space=pltpu.SMEM),
                  seq_blk(COL_QB), seq_blk(COL_KB), seq_blk(COL_VB), meta_blk(COL_KB), meta_blk(COL_VB),
                  pl.BlockSpec((n_rows, 1), lambda b, p, *_: (0, 0)),
                  new_blk(COL_QB), new_blk(COL_KB), new_blk(COL_VB),
                  pl.BlockSpec(memory_space=pl.ANY), pl.BlockSpec(memory_space=pl.ANY)],
        out_specs=[pl.BlockSpec((None, seq_len, LANES), lambda b, p, *_: (b, 0, p)),
                   pl.BlockSpec((step_rows, width), lambda b, p, *_: (b * n_pairs + p, 0))],
        scratch_shapes=[pltpu.VMEM((n_main, pair_rows, LANES), BF16),
                        pltpu.VMEM((n_main + 1, tile, LANES), BF16),
                        pltpu.VMEM((n_main + 1, pair_rows, LANES), BF16),
                        pltpu.VMEM((4, pair_rows, tile), F32),
                        pltpu.VMEM((tile, tile), BF16),
                        pltpu.VMEM((2, pair_rows, tile), F32),
                        pltpu.VMEM((2, pair_rows, tile), F32),
                        pltpu.VMEM((2, pair_rows, tile), F32),
                        pltpu.VMEM((2, pair_rows, LANES), F32),
                        pltpu.VMEM((pair_rows, LANES), F32),
                        pltpu.VMEM((tile, LANES), F32),
                        pltpu.VMEM((seqs_per_step, n_rows, width), BF16),
                        pltpu.VMEM((seqs_per_step, n_rows, width), F32),
                        pltpu.VMEM((seqs_per_step, n_rows, 1), F32),
                        pltpu.VMEM((page_size, page_size), BF16),
                        pltpu.VMEM((ring, width, page_size), F32),
                        pltpu.VMEM((ring, width, page_size), F32),
                        pltpu.SemaphoreType.DMA((2, ring))],
    )
    out_p, out_s = pl.pallas_call(
        functools.partial(_sb_kernel, tile=tile, n_main=n_main, n_visits=n_visits, n_tok=n_tok,
                          seqs_per_step=seqs_per_step, n_pages=n_pages),
        grid_spec=grid_spec,
        out_shape=[jax.ShapeDtypeStruct((n_seq, seq_len, width), BF16),
                   jax.ShapeDtypeStruct((n_dec * n_tok, width), F32)],
        compiler_params=pltpu.CompilerParams(
            dimension_semantics=("arbitrary", "arbitrary"), vmem_limit_bytes=VMEM_LIMIT),
    )(vq, vt, page_table, sb_bias, p3, p3, p3, proj_meta, proj_meta, bias_rows, proj_s, proj_s, proj_s, ck, cv)
    return out_p.reshape(n_seq * seq_len, width), out_s


def _merge_kernel(oa_ref, ob_ref, ga_ref, gb_ref, h_ref, woa_ref, wob_ref, wout_ref, g_ref, out_ref):
    pa = _dot(oa_ref[...].astype(BF16), woa_ref[...])
    pb = _dot(ob_ref[...].astype(BF16), wob_ref[...])
    mixed = jax.nn.sigmoid(ga_ref[...]) * pa + jax.nn.sigmoid(gb_ref[...]) * pb
    out_ref[...] = h_ref[...] + _rms(_dot(mixed.astype(BF16), wout_ref[...]), g_ref[...])


def _merge(o_a, o_b, proj, h, w_o_gla, w_o_sb, w_out, g_post, *, tm):
    m, d = h.shape
    row = pl.BlockSpec((tm, d), lambda i: (i, 0))
    return pl.pallas_call(
        _merge_kernel,
        grid=(m // tm,),
        in_specs=[row, row,
                  pl.BlockSpec((tm, d), lambda i: (i, COL_GA // d)),
                  pl.BlockSpec((tm, d), lambda i: (i, COL_GB // d)),
                  row, _const_spec(w_o_gla.shape), _const_spec(w_o_sb.shape), _const_spec(w_out.shape),
                  _const_spec((1, d))],
        out_specs=row,
        out_shape=jax.ShapeDtypeStruct((m, d), F32),
        compiler_params=pltpu.CompilerParams(
            dimension_semantics=("arbitrary",), vmem_limit_bytes=VMEM_LIMIT),
    )(o_a, o_b, proj, proj, h, w_o_gla, w_o_sb, w_out, g_post)


def _row_tile(m, cap):
    return m if m <= cap else cap


def kernel(x_prompt, x_sample, cache_k, cache_v, state_gla, page_table, meta_tokens,
           ffn1_pre_g, ffn1_w_gate, ffn1_w_up, ffn1_w_down, ffn1_post_g,
           mix_pre_g, w_in, w_gk2, b_gk, gla_norm_g, sb_bias, w_o_gla, w_o_sb, w_out, mix_post_g,
           ffn2_pre_g, ffn2_w_gate, ffn2_w_up, ffn2_w_down, ffn2_post_g):
    n_seq, seq_len, d = x_prompt.shape
    n_dec, n_tok, _ = x_sample.shape
    assert ffn1_pre_g.shape[0] == 1, "single layer"
    qk = GLA_HEADS * GLA_DK
    v_w = GLA_HEADS * GLA_DV
    rank = w_gk2.shape[1]

    bf = lambda w: w[0].astype(BF16)
    w1g, w1u, w1d = bf(ffn1_w_gate), bf(ffn1_w_up), bf(ffn1_w_down)
    w2g, w2u, w2d = bf(ffn2_w_gate), bf(ffn2_w_up), bf(ffn2_w_down)
    w_in0 = w_in[0]
    gk0 = 2 * qk + v_w
    w_main = jnp.concatenate([w_in0[:, :gk0], w_in0[:, gk0 + rank:]], axis=1).astype(BF16)
    w_gk1 = jnp.pad(w_in0[:, gk0:gk0 + rank], ((0, 0), (0, LANES - rank))).astype(BF16)
    w_gk2p = jnp.pad(w_gk2[0], ((0, LANES - rank), (0, 0))).astype(BF16)
    woa, wob, wout = bf(w_o_gla), bf(w_o_sb), bf(w_out)
    row = lambda g: g[0].reshape(1, -1)
    bias = sb_bias[0]

    def pre_mix(x, tm):
        h, u = _ffn(x, row(ffn1_pre_g), w1g, w1u, w1d, row(ffn1_post_g), row(mix_pre_g), tm=tm)
        proj, glog = _proj(u, w_main, w_gk1, w_gk2p, row(b_gk), tm=_row_tile(x.shape[0], 2048), tn=1024)
        return h, u, proj, glog

    def post_mix(h, proj, o_a, o_b, tm):
        h2 = _merge(o_a, o_b, proj, h, woa, wob, wout, row(mix_post_g), tm=tm)
        return _ffn(h2, row(ffn2_pre_g), w2g, w2u, w2d, row(ffn2_post_g), tm=tm)

    x_meta = jnp.pad(meta_tokens.astype(F32), ((0, LANES - N_META), (0, 0)))
    _, u_m, proj_m, glog_m = pre_mix(x_meta, LANES)

    m_p = n_seq * seq_len
    tm_p = _row_tile(m_p, 512)
    h_p, u_p, proj_p, glog_p = pre_mix(x_prompt.reshape(m_p, d), tm_p)
    oa_p, s_p = _gla(proj_p, glog_p, row(gla_norm_g), n_seq=n_seq, seq_len=seq_len,
                     rows_per_step=_row_tile(seq_len, 512), chunk=GLA_CHUNK, n_heads=4,
                     meta=(proj_m, glog_m))

    m_s = n_dec * n_tok
    tm_s = _row_tile(m_s, 512)
    h_s, _, proj_s, glog_s = pre_mix(x_sample.reshape(m_s, d), tm_s)

    ob_p, ob_s = _sb_attention(proj_p, proj_m, proj_s, cache_k[0], cache_v[0], page_table, bias,
                               n_seq=n_seq, seq_len=seq_len, tile=MXU_DIM, n_dec=n_dec, n_tok=n_tok)
    y_p = post_mix(h_p, proj_p, oa_p, ob_p, tm_p)

    pad_tok = lambda a: jnp.pad(a.reshape(n_dec, n_tok, -1), ((0, 0), (0, SUBLANES - n_tok), (0, 0))
                                ).reshape(n_dec * SUBLANES, -1)
    oa_s, s_s = _gla(pad_tok(proj_s[:, :COL_QB]), pad_tok(glog_s), row(gla_norm_g), n_seq=n_dec,
                     seq_len=SUBLANES, rows_per_step=SUBLANES, chunk=SUBLANES, n_heads=GLA_HEADS,
                     n_seqs=4, state0=state_gla[0])
    oa_s = oa_s.reshape(n_dec, SUBLANES, v_w)[:, :n_tok].reshape(m_s, v_w)
    y_s = post_mix(h_s, proj_s, oa_s, ob_s, tm_s)

    heads = (SB_HEADS, SB_HEAD_DIM)
    w_sb = SB_HEADS * SB_HEAD_DIM

    u_full = jnp.concatenate([jnp.broadcast_to(u_m[:N_META][None], (n_seq, N_META, d)),
                              u_p.reshape(n_seq, seq_len, d)], axis=1)
    k_t, v_t = _kv_t(w_main[:, COL_KB:COL_KB + w_sb].T, w_main[:, COL_VB:COL_VB + w_sb].T, u_full,
                     tp=512)
    kv_out = lambda a: a.reshape(n_seq, *heads, N_META + seq_len).transpose(0, 3, 1, 2)[None]

    return (y_p.reshape(n_seq, seq_len, d),
            y_s.reshape(n_dec, n_tok, d),
            kv_out(k_t), kv_out(v_t),
            s_p[None],
            proj_s[:, COL_KB:COL_KB + w_sb].reshape(1, n_dec, n_tok, *heads),
            proj_s[:, COL_VB:COL_VB + w_sb].reshape(1, n_dec, n_tok, *heads),
            s_s[None])
```

```python
import functools

import jax
import jax.numpy as jnp
from jax import lax
from jax.experimental import pallas as pl
from jax.experimental.pallas import tpu as pltpu

F32 = jnp.float32
BF16 = jnp.bfloat16

EPS = 1e-6
N_META = 16
GLA_HEADS = 4
GLA_DK = 128
GLA_DV = 256
GLA_TAU = 16.0
GLA_CHUNK = 64
SB_HEADS = 16
SB_HEAD_DIM = 64
LANES = 128
SUBLANES = 8
MXU_DIM = 256
HEADS_PER_SLAB = LANES // SB_HEAD_DIM
VMEM_LIMIT = 56 * 1024 * 1024
NEG_BIG = -1e30
LOG2E = 1.4426950408889634
FFN_ROW_GROUPS = 2
SB_VISITS_PER_TRIP = 4
SB_PAGES_PER_TRIP = 2
SB_PAGE_GROUP = 2
SB_PREFETCH_TRIPS = 4

COL_QA, COL_KA, COL_VA, COL_RA = 0, 512, 1024, 2048
COL_QB, COL_KB, COL_VB, COL_GA, COL_GB = 3072, 4096, 5120, 6144, 7168
PROJ_W = 8192


def _rms(x, g):
    return x * lax.rsqrt(jnp.mean(x * x, axis=-1, keepdims=True) + EPS) * g


def _const_spec(shape):
    return pl.BlockSpec(shape, lambda *_: (0,) * len(shape), pipeline_mode=pl.Buffered(1))


def _dot(a, b):
    return jnp.dot(a, b, preferred_element_type=F32)


def _dot_nt(a, b):
    return lax.dot_general(a, b, (((1,), (1,)), ((), ())), preferred_element_type=F32)


def _dot_tn(a, b):
    return lax.dot_general(a, b, (((0,), (0,)), ((), ())), preferred_element_type=F32)


def _split3(x):
    hi = x.astype(BF16)
    r1 = x - hi.astype(F32)
    mid = r1.astype(BF16)
    lo = (r1 - mid.astype(F32)).astype(BF16)
    return hi, mid, lo


def _ffn_kernel(x_ref, gpre_ref, wg_ref, wu_ref, wd_ref, gpost_ref, *rest, ff_chunk, emit_u):
    if emit_u:
        gnext_ref, h_ref, u_ref, act_ref = rest
    else:
        h_ref, act_ref = rest
    tm = x_ref.shape[0]
    n_sub = FFN_ROW_GROUPS if tm % (FFN_ROW_GROUPS * SUBLANES) == 0 else 1
    subs = [slice(s * tm // n_sub, (s + 1) * tm // n_sub) for s in range(n_sub)]
    xs = [x_ref[rs, :] for rs in subs]
    us = [_rms(x, gpre_ref[...]).astype(BF16) for x in xs]
    d_ff = wg_ref.shape[1]
    for c in range(d_ff // ff_chunk):
        sl = slice(c * ff_chunk, (c + 1) * ff_chunk)
        for rs, u in zip(subs, us):
            gate = _dot(u, wg_ref[:, sl])
            up = _dot(u, wu_ref[:, sl])
            act_ref[rs, sl] = (gate * jax.nn.sigmoid(gate) * up).astype(BF16)
    ys = [_dot(act_ref[rs, :], wd_ref[...]) for rs in subs]
    for rs, x, y in zip(subs, xs, ys):
        h = x + 0.5 * _rms(y, gpost_ref[...])
        h_ref[rs, :] = h
        if emit_u:
            u_ref[rs, :] = _rms(h, gnext_ref[...]).astype(BF16)


def _ffn(x, g_pre, wg, wu, wd, g_post, g_next=None, *, tm):
    m, d = x.shape
    d_ff = wg.shape[1]
    emit_u = g_next is not None
    row = pl.BlockSpec((tm, d), lambda i: (i, 0))
    in_specs = [row, _const_spec((1, d)), _const_spec((d, d_ff)), _const_spec((d, d_ff)),
                _const_spec((d_ff, d)), _const_spec((1, d))]
    args = [x, g_pre, wg, wu, wd, g_post]
    out_shape = [jax.ShapeDtypeStruct((m, d), F32)]
    out_specs = [row]
    if emit_u:
        in_specs.append(_const_spec((1, d)))
        args.append(g_next)
        out_shape.append(jax.ShapeDtypeStruct((m, d), BF16))
        out_specs.append(row)
    outs = pl.pallas_call(
        functools.partial(_ffn_kernel, ff_chunk=MXU_DIM, emit_u=emit_u),
        grid=(m // tm,),
        in_specs=in_specs,
        out_specs=out_specs,
        out_shape=out_shape,
        scratch_shapes=[pltpu.VMEM((tm, d_ff), BF16)],
        compiler_params=pltpu.CompilerParams(
            dimension_semantics=("arbitrary",), vmem_limit_bytes=VMEM_LIMIT),
    )(*args)
    return outs if emit_u else outs[0]


def _proj_kernel(u_ref, w_ref, wgk1_ref, wgk2_ref, bgk_ref, p_ref, g_ref):
    u = u_ref[...]
    p_ref[...] = _dot(u, w_ref[...])

    @pl.when(pl.program_id(1) == 0)
    def _():
        low = _dot(u, wgk1_ref[...]).astype(BF16)
        zg = _dot(low, wgk2_ref[...]) + bgk_ref[...]
        g_ref[...] = (jnp.minimum(zg, 0.0) - jnp.log(1.0 + jnp.exp(-jnp.abs(zg)))) * (1.0 / GLA_TAU)


def _proj(u, w_main, w_gk1, w_gk2, b_gk, *, tm, tn):
    m, d = u.shape
    n = w_main.shape[1]
    qk = w_gk2.shape[1]
    return pl.pallas_call(
        _proj_kernel,
        grid=(m // tm, n // tn),
        in_specs=[pl.BlockSpec((tm, d), lambda i, j: (i, 0)),
                  pl.BlockSpec((d, tn), lambda i, j: (0, j)),
                  _const_spec(w_gk1.shape), _const_spec(w_gk2.shape), _const_spec((1, qk))],
        out_specs=[pl.BlockSpec((tm, tn), lambda i, j: (i, j)),
                   pl.BlockSpec((tm, qk), lambda i, j: (i, 0))],
        out_shape=[jax.ShapeDtypeStruct((m, n), F32), jax.ShapeDtypeStruct((m, qk), F32)],
        compiler_params=pltpu.CompilerParams(
            dimension_semantics=("arbitrary", "arbitrary"), vmem_limit_bytes=VMEM_LIMIT),
    )(u, w_main, w_gk1, w_gk2, b_gk)


def _kv_t_kernel(wk_ref, wv_ref, u_ref, k_ref, v_ref):
    u = u_ref[...]
    k_ref[...] = _dot_nt(wk_ref[...], u)
    v_ref[...] = _dot_nt(wv_ref[...], u)


def _kv_t(wk_t, wv_t, u_full, *, tp):
    width, d = wk_t.shape
    n_seq, n_pos, _ = u_full.shape
    out = pl.BlockSpec((None, width, tp), lambda b, j: (b, 0, j))
    return pl.pallas_call(
        _kv_t_kernel,
        grid=(n_seq, pl.cdiv(n_pos, tp)),
        in_specs=[_const_spec((width, d)), _const_spec((width, d)),
                  pl.BlockSpec((None, tp, d), lambda b, j: (b, j, 0))],
        out_specs=[out, out],
        out_shape=[jax.ShapeDtypeStruct((n_seq, width, n_pos), F32)] * 2,
        compiler_params=pltpu.CompilerParams(
            dimension_semantics=("arbitrary", "arbitrary"), vmem_limit_bytes=VMEM_LIMIT),
    )(wk_t, wv_t, u_full)


def _gla_local(problems, c):
    rows = lax.broadcasted_iota(jnp.int32, (c, c), 0)
    cols = lax.broadcasted_iota(jnp.int32, (c, c), 1)
    causal = cols <= rows
    tri = causal.astype(BF16)
    ones_c = jnp.ones((c, LANES), BF16)
    parts = [_split3(g) for _, _, _, g in problems]
    cums = [_dot(tri, p[0]) + _dot(tri, p[1]) + _dot(tri, p[2]) for p in parts]
    sums = [_dot_tn(p[0], ones_c) + _dot_tn(p[1], ones_c) + _dot_tn(p[2], ones_c) for p in parts]
    q_ins, k_ins, k_ends, vbs = [], [], [], []
    for (q, k, v, _), b in zip(problems, cums):
        q_ins.append((q * (GLA_DK ** -0.5) * jnp.exp(b)).astype(BF16))
        k_ins.append((k * jnp.exp(-b)).astype(BF16))
        k_ends.append((k * jnp.exp(b[c - 1:c, :] - b)).astype(BF16))
        vbs.append(v.astype(BF16))
    scores = [_dot_nt(qi, ki) for qi, ki in zip(q_ins, k_ins)]
    o_intra = [_dot(jnp.where(causal, a, 0.0).astype(BF16), vb) for a, vb in zip(scores, vbs)]
    d_state = [_dot_tn(ke, vb) for ke, vb in zip(k_ends, vbs)]
    decays = [jnp.concatenate([jnp.exp(bs)] * (GLA_DV // LANES), axis=1) for bs in sums]
    return list(zip(q_ins, o_intra, d_state, decays))


def _gla_kernel(*refs, n_seqs, n_heads, chunk, n_chunks, lead):
    if lead:
        km_ref, vm_ref, gm_ref = refs[:3]
        refs = refs[3:]
    else:
        s0_ref = refs[0]
        refs = refs[1:]
    q_ref, k_ref, v_ref, g_ref, r_ref, gn_ref, o_ref, sfin_ref, s_scr = refs
    step = pl.program_id(2)
    dk = lambda h: slice(h * GLA_DK, (h + 1) * GLA_DK)
    dv = lambda h: slice(h * GLA_DV, (h + 1) * GLA_DV)
    rows = lambda sq, ci: slice((sq * n_chunks + ci) * chunk, (sq * n_chunks + ci + 1) * chunk)
    chains = [(sq, h) for sq in range(n_seqs) for h in range(n_heads)]

    @pl.when(step == 0)
    def _():
        if lead:
            zeros = jnp.zeros((lead, GLA_DK), F32)
            metas = _gla_local([(zeros, km_ref[:, dk(h)], vm_ref[:, dv(h)], gm_ref[:, dk(h)])
                                for h in range(n_heads)], lead)
            for h in range(n_heads):
                s_scr[0, h] = metas[h][2]
        else:
            s_scr[...] = s0_ref[...]

    problems = []
    for sq, h in chains:
        for ci in range(n_chunks):
            sl = rows(sq, ci)
            problems.append((q_ref[sl, dk(h)], k_ref[sl, dk(h)], v_ref[sl, dv(h)], g_ref[sl, dk(h)]))
    local = _gla_local(problems, chunk)
    gn = gn_ref[...]
    for c, (sq, h) in enumerate(chains):
        s = s_scr[sq, h]
        for ci in range(n_chunks):
            sl = rows(sq, ci)
            q_in, o_intra, d_state, decay = local[c * n_chunks + ci]
            o = o_intra + _dot(q_in, s.astype(BF16))
            s = decay * s + d_state
            r = r_ref[sl, dv(h)]
            o_ref[sl, dv(h)] = (_rms(o, gn) * (r * jax.nn.sigmoid(r))).astype(BF16)
        s_scr[sq, h] = s

    @pl.when(step == pl.num_programs(2) - 1)
    def _():
        sfin_ref[...] = s_scr[...]


def _gla(proj, glog, gla_norm_g, *, n_seq, seq_len, rows_per_step, chunk, n_heads, n_seqs=1, meta=None,
         state0=None):
    steps = seq_len // rows_per_step
    n_chunks = rows_per_step // chunk
    groups = GLA_HEADS // n_heads
    wk, wv = n_heads * GLA_DK, n_heads * GLA_DV
    assert n_seqs == 1 or (steps == 1 and meta is None)
    assert n_seq % n_seqs == 0

    def rows(width, col0):
        return pl.BlockSpec((n_seqs * rows_per_step, width),
                            lambda b, h, s: (b * steps + s, col0 // width + h))

    state_spec = pl.BlockSpec((n_seqs, n_heads, GLA_DK, GLA_DV), lambda b, h, s: (b, h, 0, 0))
    in_specs, args = [], []
    if meta is not None:
        pm, gm = meta
        in_specs += [pl.BlockSpec((N_META, wk), lambda b, h, s: (0, COL_KA // wk + h)),
                     pl.BlockSpec((N_META, wv), lambda b, h, s: (0, COL_VA // wv + h)),
                     pl.BlockSpec((N_META, wk), lambda b, h, s: (0, h))]
        args += [pm, pm, gm]
    else:
        in_specs.append(state_spec)
        args.append(state0)
    in_specs += [rows(wk, COL_QA), rows(wk, COL_KA), rows(wv, COL_VA), rows(wk, 0), rows(wv, COL_RA),
                 _const_spec((1, GLA_DV))]
    args += [proj, proj, proj, glog, proj, gla_norm_g]
    m = n_seq * seq_len
    return pl.pallas_call(
        functools.partial(_gla_kernel, n_seqs=n_seqs, n_heads=n_heads, chunk=chunk, n_chunks=n_chunks,
                          lead=N_META if meta is not None else 0),
        grid=(n_seq // n_seqs, groups, steps),
        in_specs=in_specs,
        out_specs=[rows(wv, 0), state_spec],
        out_shape=[jax.ShapeDtypeStruct((m, GLA_HEADS * GLA_DV), BF16),
                   jax.ShapeDtypeStruct((n_seq, GLA_HEADS, GLA_DK, GLA_DV), F32)],
        scratch_shapes=[pltpu.VMEM((n_seqs, n_heads, GLA_DK, GLA_DV), F32)],
        compiler_params=pltpu.CompilerParams(
            dimension_semantics=("arbitrary", "arbitrary", "arbitrary"), vmem_limit_bytes=VMEM_LIMIT),
    )(*args)


def _sb_logs(zb):
    nl = jnp.maximum(zb, 0.0) + jnp.log(1.0 + jnp.exp2(jnp.abs(zb) * (-LOG2E)))
    return zb - nl, nl


def _suffix_matrix(t):
    rows = lax.broadcasted_iota(jnp.int32, (t, t), 0)
    cols = lax.broadcasted_iota(jnp.int32, (t, t), 1)
    return (rows > cols).astype(BF16)


def _sb_suffix(nl, suffix):
    later = _dot(nl.astype(BF16), suffix)
    return later, jnp.sum(nl, axis=1, keepdims=True)


def _own_head_mask(n_rows):
    width = SB_HEADS * SB_HEAD_DIM
    row_id = lax.broadcasted_iota(jnp.int32, (n_rows, width), 0)
    col_id = lax.broadcasted_iota(jnp.int32, (n_rows, width), 1)
    return lax.shift_right_logical(col_id, SB_HEAD_DIM.bit_length() - 1) == (row_id & (SB_HEADS - 1))


def _sb_kernel(vq_ref, vt_ref, pt_ref, bias_ref, q_ref, k_ref, v_ref, km_ref, vm_ref,
               sbias_ref, sq_ref, skn_ref, svn_ref, ck_hbm, cv_hbm,
               o_ref, so_ref,
               q_scr, k_scr, vv_scr, bt_scr, sfx_scr, z_scr, w_scr, later_scr, tot_scr, nrun_scr, acc_scr,
               sqbd_scr, sacc_scr, srun_scr, spage_sfx_scr, kbuf, vbuf, sems,
               *, tile, n_main, n_visits, n_tok, seqs_per_step, n_pages):
    pair = pl.program_id(1)
    low_lanes = lax.broadcasted_iota(jnp.int32, (1, LANES), 1) < SB_HEAD_DIM
    both = lambda x: jnp.concatenate([jnp.where(low_lanes, x, 0.0), jnp.where(low_lanes, 0.0, x)], axis=0)
    DIAG, FULL, META, NONE = range(4)

    for j in range(n_main):
        sl = slice(j * tile, (j + 1) * tile)
        q_scr[j] = both(q_ref[sl, :] * (SB_HEAD_DIM ** -0.5)).astype(BF16)
        k_scr[j] = k_ref[sl, :].astype(BF16)
        vv_scr[j] = both(v_ref[sl, :]).astype(BF16)
    pad = jnp.zeros((tile - LANES, LANES), F32)
    k_scr[n_main] = jnp.concatenate([km_ref[...], pad], axis=0).astype(BF16)
    vv_scr[n_main] = both(jnp.concatenate([vm_ref[...], pad], axis=0)).astype(BF16)
    rows = lax.broadcasted_iota(jnp.int32, (tile, tile), 0)
    cols = lax.broadcasted_iota(jnp.int32, (tile, tile), 1)
    kinds = {DIAG: jnp.where(cols < rows, 0.0, NEG_BIG), FULL: jnp.zeros((tile, tile), F32),
             META: jnp.where(cols < N_META, 0.0, NEG_BIG), NONE: jnp.full((tile, tile), NEG_BIG, F32)}
    for kind, mask in kinds.items():
        for h in range(HEADS_PER_SLAB):
            bt_scr[kind, h * tile:(h + 1) * tile, :] = mask + bias_ref[HEADS_PER_SLAB * pair + h]
    sfx_scr[...] = _suffix_matrix(tile)

    def key_slot(q, t):
        return jnp.where(t <= q, q - t, n_main)

    def kind_of(q, t):
        return jnp.where(t == 0, DIAG, jnp.where(t <= q, FULL, jnp.where(t == q + 1, META, NONE)))

    def logits(v):
        q, t = vq_ref[v], vt_ref[v]
        return _dot_nt(q_scr[q], k_scr[key_slot(q, t)])

    def stage(v, cur, nxt):
        z_scr[nxt] = logits(v + 1)
        w, nl = _sb_logs(z_scr[cur] + bt_scr[kind_of(vq_ref[v], vt_ref[v])])
        later, tot = _sb_suffix(nl, sfx_scr[...])
        prev = jnp.maximum(v - 1, 0)
        q, t = vq_ref[prev], vt_ref[prev]
        fresh = t == 0
        nrun = jnp.where(fresh, 0.0, nrun_scr[...])
        a = jnp.exp(w_scr[nxt] - later_scr[nxt] - jnp.concatenate([nrun, nrun], axis=1)).astype(BF16)
        acc = jnp.where(fresh, 0.0, acc_scr[...]) + _dot(jnp.concatenate([a[:tile], a[tile:]], axis=1),
                                                          vv_scr[key_slot(q, t)])
        acc_scr[...] = acc
        o_ref[pl.ds(pl.multiple_of(q * tile, tile), tile), :] = acc.astype(BF16)
        nrun_scr[...] = nrun + tot_scr[nxt]
        w_scr[cur] = w
        later_scr[cur] = later
        tot_scr[cur] = jnp.broadcast_to(tot, tot_scr.shape[1:])

    z_scr[0] = logits(0)
    w_scr[1] = jnp.full(w_scr.shape[1:], NEG_BIG, F32)
    later_scr[1] = jnp.zeros(later_scr.shape[1:], F32)
    tot_scr[1] = jnp.zeros(tot_scr.shape[1:], F32)
    nrun_scr[...] = jnp.zeros(nrun_scr.shape, F32)
    acc_scr[...] = jnp.zeros(acc_scr.shape, F32)

    n_rows = n_tok * SB_HEADS
    width = SB_HEADS * SB_HEAD_DIM
    own_head = _own_head_mask(n_rows)
    sbias = sbias_ref[...]
    tok_of_row = lax.shift_right_logical(lax.broadcasted_iota(jnp.int32, (n_rows, 1), 0),
                                         SB_HEADS.bit_length() - 1)
    for s in range(seqs_per_step):
        r0 = s * n_tok
        q_rows = jnp.concatenate(
            [jnp.broadcast_to(sq_ref[r0 + t:r0 + t + 1, :] * (SB_HEAD_DIM ** -0.5), (SB_HEADS, width))
             for t in range(n_tok)], axis=0)
        qbd = jnp.where(own_head, q_rows, 0.0).astype(BF16)
        sqbd_scr[s] = qbd
        qf = qbd.astype(F32)
        nrun = jnp.zeros((n_rows, 1), F32)
        acc = jnp.zeros((n_rows, width), F32)
        for j in reversed(range(n_tok)):
            kj = skn_ref[r0 + j:r0 + j + 1, :].astype(BF16).astype(F32)
            vj = svn_ref[r0 + j:r0 + j + 1, :].astype(BF16).astype(F32)
            z = jnp.sum(qf * kj, axis=-1, keepdims=True) + sbias
            w, nl = _sb_logs(jnp.where(tok_of_row > j, z, NEG_BIG))
            acc = acc + jnp.exp(w - nrun) * vj
            nrun = nrun + nl
        sacc_scr[s] = acc
        srun_scr[s] = nrun
    spage_sfx_scr[...] = _suffix_matrix(kbuf.shape[2])

    trips = pl.cdiv(n_visits + 1, SB_VISITS_PER_TRIP)
    per_trip = SB_PAGES_PER_TRIP
    n_total = seqs_per_step * n_pages
    ring = kbuf.shape[0]
    seq0 = (pl.program_id(0) * pl.num_programs(1) + pair) * seqs_per_step

    def page_copies(g):
        g = jnp.asarray(g, jnp.int32)
        gc = jnp.minimum(g, n_total - 1)
        pool = pt_ref[seq0 + lax.div(gc, jnp.int32(n_pages)), n_pages - 1 - lax.rem(gc, jnp.int32(n_pages))]
        slot = lax.rem(g, jnp.int32(ring))
        return (pltpu.make_async_copy(ck_hbm.at[pool], kbuf.at[slot], sems.at[0, slot]),
                pltpu.make_async_copy(cv_hbm.at[pool], vbuf.at[slot], sems.at[1, slot]))

    def start_pages(g0):
        for i in range(per_trip):
            for copy in page_copies(g0 + i):
                copy.start()

    def wait_pages(g0):
        for i in range(per_trip):
            for copy in page_copies(g0 + i):
                copy.wait()

    def page_logits(g0):
        seq = lax.div(jnp.minimum(g0, n_total - 1), jnp.int32(n_pages))
        off = sbias + jnp.where(g0 < n_total, 0.0, NEG_BIG)
        return [_dot(sqbd_scr[seq], kbuf[lax.rem(g0 + i, jnp.int32(ring))].astype(BF16)) + off
                for i in range(SB_PAGE_GROUP)]

    def page_suffix(zs):
        logs = [_sb_logs(z) for z in zs]
        sums = [_sb_suffix(nl, spage_sfx_scr[...]) for _, nl in logs]
        return [(w, later, tot) for (w, _), (later, tot) in zip(logs, sums)]

    def page_finish(g0, parts):
        seq = lax.div(jnp.minimum(g0, n_total - 1), jnp.int32(n_pages))
        nrun = srun_scr[seq]
        acc = sacc_scr[seq]
        for i, (w, later, tot) in enumerate(parts):
            a = jnp.exp(w - later - nrun).astype(BF16)
            acc = acc + _dot_nt(a, vbuf[lax.rem(g0 + i, jnp.int32(ring))].astype(BF16))
            nrun = nrun + tot
        sacc_scr[seq] = acc
        srun_scr[seq] = nrun

    for ahead in range(SB_PREFETCH_TRIPS):
        start_pages(ahead * per_trip)

    groups = per_trip // SB_PAGE_GROUP
    visits_per_group = SB_VISITS_PER_TRIP // groups

    def body(u, carry):
        wait_pages(u * per_trip)
        start_pages((u + SB_PREFETCH_TRIPS) * per_trip)
        for k in range(groups):
            v0 = SB_VISITS_PER_TRIP * u + k * visits_per_group
            g0 = u * per_trip + k * SB_PAGE_GROUP
            zs = page_logits(g0)
            stage(v0, 0, 1)
            parts = page_suffix(zs)
            stage(v0 + 1, 1, 0)
            page_finish(g0, parts)
            for i in range(2, visits_per_group):
                stage(v0 + i, i % 2, (i + 1) % 2)
        return carry

    lax.fori_loop(0, trips, body, 0)
    for ahead in range(SB_PREFETCH_TRIPS):
        wait_pages((trips + ahead) * per_trip)

    for s in range(seqs_per_step):
        picked = jnp.where(own_head, sacc_scr[s], 0.0)
        for t in range(n_tok):
            so_ref[s * n_tok + t:s * n_tok + t + 1, :] = jnp.sum(
                picked[t * SB_HEADS:(t + 1) * SB_HEADS, :], axis=0, keepdims=True)


def _sb_attention(proj, proj_meta, proj_s, cache_k, cache_v, page_table, sb_bias, *, n_seq, seq_len, tile,
                  n_dec, n_tok):
    p3 = proj.reshape(n_seq, seq_len, PROJ_W)
    slab = lambda col: col // LANES
    n_main = seq_len // tile
    pair_rows = HEADS_PER_SLAB * tile
    n_pairs = SB_HEADS // HEADS_PER_SLAB
    width = SB_HEADS * SB_HEAD_DIM
    n_pages = page_table.shape[1]
    page_size = cache_k.shape[1]
    seqs_per_step = n_dec // (n_seq * n_pairs)
    assert seqs_per_step * n_seq * n_pairs == n_dec and n_pages % SB_PAGES_PER_TRIP == 0
    step_rows = seqs_per_step * n_tok
    n_rows = n_tok * SB_HEADS
    ck = cache_k.transpose(0, 2, 3, 1).reshape(cache_k.shape[0], width, page_size)
    cv = cache_v.transpose(0, 2, 3, 1).reshape(cache_v.shape[0], width, page_size)
    bias_rows = jnp.tile(sb_bias, n_tok).reshape(n_rows, 1)
    ring = (SB_PREFETCH_TRIPS + 1) * SB_PAGES_PER_TRIP
    visits = [(q, t) for q in range(n_main) for t in range(q + 2)]
    n_visits = len(visits)
    visits += [(n_main - 1, n_main + 1)] * (SB_VISITS_PER_TRIP + 2)
    vq = jnp.asarray([q for q, _ in visits], jnp.int32)
    vt = jnp.asarray([t for _, t in visits], jnp.int32)
    assert seqs_per_step * n_pages <= pl.cdiv(n_visits + 1, SB_VISITS_PER_TRIP) * SB_PAGES_PER_TRIP
    seq_blk = lambda col: pl.BlockSpec((None, seq_len, LANES), lambda b, p, *_: (b, 0, slab(col) + p))
    meta_blk = lambda col: pl.BlockSpec((LANES, LANES), lambda b, p, *_: (0, slab(col) + p))
    new_blk = lambda col: pl.BlockSpec((step_rows, width), lambda b, p, *_: (b * n_pairs + p, col // width))
    grid_spec = pltpu.PrefetchScalarGridSpec(
        num_scalar_prefetch=3,
        grid=(n_seq, n_pairs),
        in_specs=[pl.BlockSpec(memory_space=pltpu.SMEM),
                  seq_blk(COL_QB), seq_blk(COL_KB), seq_blk(COL_VB), meta_blk(COL_KB), meta_blk(COL_VB),
                  pl.BlockSpec((n_rows, 1), lambda b, p, *_: (0, 0)),
                  new_blk(COL_QB), new_blk(COL_KB), new_blk(COL_VB),
                  pl.BlockSpec(memory_space=pl.ANY), pl.BlockSpec(memory_space=pl.ANY)],
        out_specs=[pl.BlockSpec((None, seq_len, LANES), lambda b, p, *_: (b, 0, p)),
                   pl.BlockSpec((step_rows, width), lambda b, p, *_: (b * n_pairs + p, 0))],
        scratch_shapes=[pltpu.VMEM((n_main, pair_rows, LANES), BF16),
                        pltpu.VMEM((n_main + 1, tile, LANES), BF16),
                        pltpu.VMEM((n_main + 1, pair_rows, LANES), BF16),
                        pltpu.VMEM((4, pair_rows, tile), F32),
                        pltpu.VMEM((tile, tile), BF16),
                        pltpu.VMEM((2, pair_rows, tile), F32),
                        pltpu.VMEM((2, pair_rows, tile), F32),
                        pltpu.VMEM((2, pair_rows, tile), F32),
                        pltpu.VMEM((2, pair_rows, LANES), F32),
                        pltpu.VMEM((pair_rows, LANES), F32),
                        pltpu.VMEM((tile, LANES), F32),
                        pltpu.VMEM((seqs_per_step, n_rows, width), BF16),
                        pltpu.VMEM((seqs_per_step, n_rows, width), F32),
                        pltpu.VMEM((seqs_per_step, n_rows, 1), F32),
                        pltpu.VMEM((page_size, page_size), BF16),
                        pltpu.VMEM((ring, width, page_size), F32),
                        pltpu.VMEM((ring, width, page_size), F32),
                        pltpu.SemaphoreType.DMA((2, ring))],
    )
    out_p, out_s = pl.pallas_call(
        functools.partial(_sb_kernel, tile=tile, n_main=n_main, n_visits=n_visits, n_tok=n_tok,
                          seqs_per_step=seqs_per_step, n_pages=n_pages),
        grid_spec=grid_spec,
        out_shape=[jax.ShapeDtypeStruct((n_seq, seq_len, width), BF16),
                   jax.ShapeDtypeStruct((n_dec * n_tok, width), F32)],
        compiler_params=pltpu.CompilerParams(
            dimension_semantics=("arbitrary", "arbitrary"), vmem_limit_bytes=VMEM_LIMIT),
    )(vq, vt, page_table, sb_bias, p3, p3, p3, proj_meta, proj_meta, bias_rows, proj_s, proj_s, proj_s, ck, cv)
    return out_p.reshape(n_seq * seq_len, width), out_s


def _merge_kernel(oa_ref, ob_ref, ga_ref, gb_ref, h_ref, woa_ref, wob_ref, wout_ref, g_ref, out_ref):
    pa = _dot(oa_ref[...].astype(BF16), woa_ref[...])
    pb = _dot(ob_ref[...].astype(BF16), wob_ref[...])
    mixed = jax.nn.sigmoid(ga_ref[...]) * pa + jax.nn.sigmoid(gb_ref[...]) * pb
    out_ref[...] = h_ref[...] + _rms(_dot(mixed.astype(BF16), wout_ref[...]), g_ref[...])


def _merge(o_a, o_b, proj, h, w_o_gla, w_o_sb, w_out, g_post, *, tm):
    m, d = h.shape
    row = pl.BlockSpec((tm, d), lambda i: (i, 0))
    return pl.pallas_call(
        _merge_kernel,
        grid=(m // tm,),
        in_specs=[row, row,
                  pl.BlockSpec((tm, d), lambda i: (i, COL_GA // d)),
                  pl.BlockSpec((tm, d), lambda i: (i, COL_GB // d)),
                  row, _const_spec(w_o_gla.shape), _const_spec(w_o_sb.shape), _const_spec(w_out.shape),
                  _const_spec((1, d))],
        out_specs=row,
        out_shape=jax.ShapeDtypeStruct((m, d), F32),
        compiler_params=pltpu.CompilerParams(
            dimension_semantics=("arbitrary",), vmem_limit_bytes=VMEM_LIMIT),
    )(o_a, o_b, proj, proj, h, w_o_gla, w_o_sb, w_out, g_post)


def _row_tile(m, cap):
    return m if m <= cap else cap


def kernel(x_prompt, x_sample, cache_k, cache_v, state_gla, page_table, meta_tokens,
           ffn1_pre_g, ffn1_w_gate, ffn1_w_up, ffn1_w_down, ffn1_post_g,
           mix_pre_g, w_in, w_gk2, b_gk, gla_norm_g, sb_bias, w_o_gla, w_o_sb, w_out, mix_post_g,
           ffn2_pre_g, ffn2_w_gate, ffn2_w_up, ffn2_w_down, ffn2_post_g):
    n_seq, seq_len, d = x_prompt.shape
    n_dec, n_tok, _ = x_sample.shape
    assert ffn1_pre_g.shape[0] == 1, "single layer"
    qk = GLA_HEADS * GLA_DK
    v_w = GLA_HEADS * GLA_DV
    rank = w_gk2.shape[1]

    bf = lambda w: w[0].astype(BF16)
    w1g, w1u, w1d = bf(ffn1_w_gate), bf(ffn1_w_up), bf(ffn1_w_down)
    w2g, w2u, w2d = bf(ffn2_w_gate), bf(ffn2_w_up), bf(ffn2_w_down)
    w_in0 = w_in[0]
    gk0 = 2 * qk + v_w
    w_main = jnp.concatenate([w_in0[:, :gk0], w_in0[:, gk0 + rank:]], axis=1).astype(BF16)
    w_gk1 = jnp.pad(w_in0[:, gk0:gk0 + rank], ((0, 0), (0, LANES - rank))).astype(BF16)
    w_gk2p = jnp.pad(w_gk2[0], ((0, LANES - rank), (0, 0))).astype(BF16)
    woa, wob, wout = bf(w_o_gla), bf(w_o_sb), bf(w_out)
    row = lambda g: g[0].reshape(1, -1)
    bias = sb_bias[0]

    def pre_mix(x, tm):
        h, u = _ffn(x, row(ffn1_pre_g), w1g, w1u, w1d, row(ffn1_post_g), row(mix_pre_g),
                    tm=_row_tile(x.shape[0], FFN_ROW_GROUPS * tm))
        proj, glog = _proj(u, w_main, w_gk1, w_gk2p, row(b_gk), tm=_row_tile(x.shape[0], 2048), tn=1024)
        return h, u, proj, glog

    def post_mix(h, proj, o_a, o_b, tm):
        h2 = _merge(o_a, o_b, proj, h, woa, wob, wout, row(mix_post_g), tm=tm)
        return _ffn(h2, row(ffn2_pre_g), w2g, w2u, w2d, row(ffn2_post_g),
                    tm=_row_tile(h.shape[0], FFN_ROW_GROUPS * tm))

    x_meta = jnp.pad(meta_tokens.astype(F32), ((0, LANES - N_META), (0, 0)))
    _, u_m, proj_m, glog_m = pre_mix(x_meta, LANES)

    m_p = n_seq * seq_len
    tm_p = _row_tile(m_p, 512)
    h_p, u_p, proj_p, glog_p = pre_mix(x_prompt.reshape(m_p, d), tm_p)
    oa_p, s_p = _gla(proj_p, glog_p, row(gla_norm_g), n_seq=n_seq, seq_len=seq_len,
                     rows_per_step=_row_tile(seq_len, 512), chunk=GLA_CHUNK, n_heads=4,
                     meta=(proj_m, glog_m))

    m_s = n_dec * n_tok
    tm_s = _row_tile(m_s, 512)
    h_s, _, proj_s, glog_s = pre_mix(x_sample.reshape(m_s, d), tm_s)

    ob_p, ob_s = _sb_attention(proj_p, proj_m, proj_s, cache_k[0], cache_v[0], page_table, bias,
                               n_seq=n_seq, seq_len=seq_len, tile=MXU_DIM, n_dec=n_dec, n_tok=n_tok)
    y_p = post_mix(h_p, proj_p, oa_p, ob_p, tm_p)

    pad_tok = lambda a: jnp.pad(a.reshape(n_dec, n_tok, -1), ((0, 0), (0, SUBLANES - n_tok), (0, 0))
                                ).reshape(n_dec * SUBLANES, -1)
    oa_s, s_s = _gla(pad_tok(proj_s[:, :COL_QB]), pad_tok(glog_s), row(gla_norm_g), n_seq=n_dec,
                     seq_len=SUBLANES, rows_per_step=SUBLANES, chunk=SUBLANES, n_heads=GLA_HEADS,
                     n_seqs=4, state0=state_gla[0])
    oa_s = oa_s.reshape(n_dec, SUBLANES, v_w)[:, :n_tok].reshape(m_s, v_w)
    y_s = post_mix(h_s, proj_s, oa_s, ob_s, tm_s)

    heads = (SB_HEADS, SB_HEAD_DIM)
    w_sb = SB_HEADS * SB_HEAD_DIM

    u_full = jnp.concatenate([jnp.broadcast_to(u_m[:N_META][None], (n_seq, N_META, d)),
                              u_p.reshape(n_seq, seq_len, d)], axis=1)
    k_t, v_t = _kv_t(w_main[:, COL_KB:COL_KB + w_sb].T, w_main[:, COL_VB:COL_VB + w_sb].T, u_full,
                     tp=256)
    kv_out = lambda a: a.reshape(n_seq, *heads, N_META + seq_len).transpose(0, 3, 1, 2)[None]

    return (y_p.reshape(n_seq, seq_len, d),
            y_s.reshape(n_dec, n_tok, d),
            kv_out(k_t), kv_out(v_t),
            s_p[None],
            proj_s[:, COL_KB:COL_KB + w_sb].reshape(1, n_dec, n_tok, *heads),
            proj_s[:, COL_VB:COL_VB + w_sb].reshape(1, n_dec, n_tok, *heads),
            s_s[None])
```

```python
import functools

import jax
import jax.numpy as jnp
from jax import lax
from jax.experimental import pallas as pl
from jax.experimental.pallas import tpu as pltpu

F32 = jnp.float32
BF16 = jnp.bfloat16

EPS = 1e-6
N_META = 16
GLA_HEADS = 4
GLA_DK = 128
GLA_DV = 256
GLA_TAU = 16.0
GLA_CHUNK = 64
SB_HEADS = 16
SB_HEAD_DIM = 64
LANES = 128
SUBLANES = 8
MXU_DIM = 256
HEADS_PER_SLAB = LANES // SB_HEAD_DIM
VMEM_LIMIT = 56 * 1024 * 1024
NEG_BIG = -1e30
LOG2E = 1.4426950408889634
FFN_ROW_GROUPS = 2
SB_VISITS_PER_TRIP = 4
SB_PAGES_PER_TRIP = 2
SB_PAGE_GROUP = 2
SB_PREFETCH_TRIPS = 4

COL_QA, COL_KA, COL_VA, COL_RA = 0, 512, 1024, 2048
COL_QB, COL_KB, COL_VB, COL_GA, COL_GB = 3072, 4096, 5120, 6144, 7168
PROJ_W = 8192


def _rms(x, g):
    return x * lax.rsqrt(jnp.mean(x * x, axis=-1, keepdims=True) + EPS) * g


def _const_spec(shape):
    return pl.BlockSpec(shape, lambda *_: (0,) * len(shape), pipeline_mode=pl.Buffered(1))


def _dot(a, b):
    return jnp.dot(a, b, preferred_element_type=F32)


def _dot_nt(a, b):
    return lax.dot_general(a, b, (((1,), (1,)), ((), ())), preferred_element_type=F32)


def _dot_tn(a, b):
    return lax.dot_general(a, b, (((0,), (0,)), ((), ())), preferred_element_type=F32)


def _split3(x):
    hi = x.astype(BF16)
    r1 = x - hi.astype(F32)
    mid = r1.astype(BF16)
    lo = (r1 - mid.astype(F32)).astype(BF16)
    return hi, mid, lo


def _ffn_kernel(x_ref, gpre_ref, wg_ref, wu_ref, wd_ref, gpost_ref, *rest, ff_chunk, emit_u):
    if emit_u:
        gnext_ref, h_ref, u_ref, act_ref = rest
    else:
        h_ref, act_ref = rest
    tm = x_ref.shape[0]
    n_sub = FFN_ROW_GROUPS if tm % (FFN_ROW_GROUPS * SUBLANES) == 0 else 1
    subs = [slice(s * tm // n_sub, (s + 1) * tm // n_sub) for s in range(n_sub)]
    xs = [x_ref[rs, :] for rs in subs]
    us = [_rms(x, gpre_ref[...]).astype(BF16) for x in xs]
    d_ff = wg_ref.shape[1]
    for c in range(d_ff // ff_chunk):
        sl = slice(c * ff_chunk, (c + 1) * ff_chunk)
        for rs, u in zip(subs, us):
            gate = _dot(u, wg_ref[:, sl])
            up = _dot(u, wu_ref[:, sl])
            act_ref[rs, sl] = (gate * jax.nn.sigmoid(gate) * up).astype(BF16)
    ys = [_dot(act_ref[rs, :], wd_ref[...]) for rs in subs]
    for rs, x, y in zip(subs, xs, ys):
        h = x + 0.5 * _rms(y, gpost_ref[...])
        h_ref[rs, :] = h
        if emit_u:
            u_ref[rs, :] = _rms(h, gnext_ref[...]).astype(BF16)


def _ffn(x, g_pre, wg, wu, wd, g_post, g_next=None, *, tm):
    m, d = x.shape
    d_ff = wg.shape[1]
    emit_u = g_next is not None
    row = pl.BlockSpec((tm, d), lambda i: (i, 0))
    in_specs = [row, _const_spec((1, d)), _const_spec((d, d_ff)), _const_spec((d, d_ff)),
                _const_spec((d_ff, d)), _const_spec((1, d))]
    args = [x, g_pre, wg, wu, wd, g_post]
    out_shape = [jax.ShapeDtypeStruct((m, d), F32)]
    out_specs = [row]
    if emit_u:
        in_specs.append(_const_spec((1, d)))
        args.append(g_next)
        out_shape.append(jax.ShapeDtypeStruct((m, d), BF16))
        out_specs.append(row)
    outs = pl.pallas_call(
        functools.partial(_ffn_kernel, ff_chunk=MXU_DIM, emit_u=emit_u),
        grid=(m // tm,),
        in_specs=in_specs,
        out_specs=out_specs,
        out_shape=out_shape,
        scratch_shapes=[pltpu.VMEM((tm, d_ff), BF16)],
        compiler_params=pltpu.CompilerParams(
            dimension_semantics=("arbitrary",), vmem_limit_bytes=VMEM_LIMIT),
    )(*args)
    return outs if emit_u else outs[0]


def _proj_kernel(u_ref, w_ref, wgk1_ref, wgk2_ref, bgk_ref, p_ref, g_ref):
    u = u_ref[...]
    p_ref[...] = _dot(u, w_ref[...])

    @pl.when(pl.program_id(1) == 0)
    def _():
        low = _dot(u, wgk1_ref[...]).astype(BF16)
        zg = _dot(low, wgk2_ref[...]) + bgk_ref[...]
        g_ref[...] = (jnp.minimum(zg, 0.0) - jnp.log(1.0 + jnp.exp(-jnp.abs(zg)))) * (1.0 / GLA_TAU)


def _proj(u, w_main, w_gk1, w_gk2, b_gk, *, tm, tn):
    m, d = u.shape
    n = w_main.shape[1]
    qk = w_gk2.shape[1]
    return pl.pallas_call(
        _proj_kernel,
        grid=(m // tm, n // tn),
        in_specs=[pl.BlockSpec((tm, d), lambda i, j: (i, 0)),
                  pl.BlockSpec((d, tn), lambda i, j: (0, j)),
                  _const_spec(w_gk1.shape), _const_spec(w_gk2.shape), _const_spec((1, qk))],
        out_specs=[pl.BlockSpec((tm, tn), lambda i, j: (i, j)),
                   pl.BlockSpec((tm, qk), lambda i, j: (i, 0))],
        out_shape=[jax.ShapeDtypeStruct((m, n), F32), jax.ShapeDtypeStruct((m, qk), F32)],
        compiler_params=pltpu.CompilerParams(
            dimension_semantics=("arbitrary", "arbitrary"), vmem_limit_bytes=VMEM_LIMIT),
    )(u, w_main, w_gk1, w_gk2, b_gk)


def _kv_t_kernel(wk_ref, wv_ref, u_ref, k_ref, v_ref):
    u = u_ref[...]
    k_ref[...] = _dot_nt(wk_ref[...], u)
    v_ref[...] = _dot_nt(wv_ref[...], u)


def _kv_t(wk_t, wv_t, u_full, *, tp):
    width, d = wk_t.shape
    n_seq, n_pos, _ = u_full.shape
    out = pl.BlockSpec((None, width, tp), lambda b, j: (b, 0, j))
    return pl.pallas_call(
        _kv_t_kernel,
        grid=(n_seq, pl.cdiv(n_pos, tp)),
        in_specs=[_const_spec((width, d)), _const_spec((width, d)),
                  pl.BlockSpec((None, tp, d), lambda b, j: (b, j, 0))],
        out_specs=[out, out],
        out_shape=[jax.ShapeDtypeStruct((n_seq, width, n_pos), F32)] * 2,
        compiler_params=pltpu.CompilerParams(
            dimension_semantics=("arbitrary", "arbitrary"), vmem_limit_bytes=VMEM_LIMIT),
    )(wk_t, wv_t, u_full)


def _gla_local(problems, c):
    rows = lax.broadcasted_iota(jnp.int32, (c, c), 0)
    cols = lax.broadcasted_iota(jnp.int32, (c, c), 1)
    causal = cols <= rows
    tri = causal.astype(BF16)
    ones_c = jnp.ones((c, LANES), BF16)
    parts = [_split3(g) for _, _, _, g in problems]
    cums = [_dot(tri, p[0]) + _dot(tri, p[1]) + _dot(tri, p[2]) for p in parts]
    sums = [_dot_tn(p[0], ones_c) + _dot_tn(p[1], ones_c) + _dot_tn(p[2], ones_c) for p in parts]
    q_ins, k_ins, k_ends, vbs = [], [], [], []
    for (q, k, v, _), b in zip(problems, cums):
        q_ins.append((q * (GLA_DK ** -0.5) * jnp.exp(b)).astype(BF16))
        k_ins.append((k * jnp.exp(-b)).astype(BF16))
        k_ends.append((k * jnp.exp(b[c - 1:c, :] - b)).astype(BF16))
        vbs.append(v.astype(BF16))
    scores = [_dot_nt(qi, ki) for qi, ki in zip(q_ins, k_ins)]
    o_intra = [_dot(jnp.where(causal, a, 0.0).astype(BF16), vb) for a, vb in zip(scores, vbs)]
    d_state = [_dot_tn(ke, vb) for ke, vb in zip(k_ends, vbs)]
    decays = [jnp.concatenate([jnp.exp(bs)] * (GLA_DV // LANES), axis=1) for bs in sums]
    return list(zip(q_ins, o_intra, d_state, decays))


def _gla_kernel(*refs, n_seqs, n_heads, chunk, n_chunks, lead):
    if lead:
        km_ref, vm_ref, gm_ref = refs[:3]
        refs = refs[3:]
    else:
        s0_ref = refs[0]
        refs = refs[1:]
    q_ref, k_ref, v_ref, g_ref, r_ref, gn_ref, o_ref, sfin_ref, s_scr = refs
    step = pl.program_id(2)
    dk = lambda h: slice(h * GLA_DK, (h + 1) * GLA_DK)
    dv = lambda h: slice(h * GLA_DV, (h + 1) * GLA_DV)
    rows = lambda sq, ci: slice((sq * n_chunks + ci) * chunk, (sq * n_chunks + ci + 1) * chunk)
    chains = [(sq, h) for sq in range(n_seqs) for h in range(n_heads)]

    @pl.when(step == 0)
    def _():
        if lead:
            zeros = jnp.zeros((lead, GLA_DK), F32)
            metas = _gla_local([(zeros, km_ref[:, dk(h)], vm_ref[:, dv(h)], gm_ref[:, dk(h)])
                                for h in range(n_heads)], lead)
            for h in range(n_heads):
                s_scr[0, h] = metas[h][2]
        else:
            s_scr[...] = s0_ref[...]

    problems = []
    for sq, h in chains:
        for ci in range(n_chunks):
            sl = rows(sq, ci)
            problems.append((q_ref[sl, dk(h)], k_ref[sl, dk(h)], v_ref[sl, dv(h)], g_ref[sl, dk(h)]))
    local = _gla_local(problems, chunk)
    gn = gn_ref[...]
    for c, (sq, h) in enumerate(chains):
        s = s_scr[sq, h]
        for ci in range(n_chunks):
            sl = rows(sq, ci)
            q_in, o_intra, d_state, decay = local[c * n_chunks + ci]
            o = o_intra + _dot(q_in, s.astype(BF16))
            s = decay * s + d_state
            r = r_ref[sl, dv(h)]
            o_ref[sl, dv(h)] = (_rms(o, gn) * (r * jax.nn.sigmoid(r))).astype(BF16)
        s_scr[sq, h] = s

    @pl.when(step == pl.num_programs(2) - 1)
    def _():
        sfin_ref[...] = s_scr[...]


def _gla(proj, glog, gla_norm_g, *, n_seq, seq_len, rows_per_step, chunk, n_heads, n_seqs=1, meta=None,
         state0=None):
    steps = seq_len // rows_per_step
    n_chunks = rows_per_step // chunk
    groups = GLA_HEADS // n_heads
    wk, wv = n_heads * GLA_DK, n_heads * GLA_DV
    assert n_seqs == 1 or (steps == 1 and meta is None)
    assert n_seq % n_seqs == 0

    def rows(width, col0):
        return pl.BlockSpec((n_seqs * rows_per_step, width),
                            lambda b, h, s: (b * steps + s, col0 // width + h))

    state_spec = pl.BlockSpec((n_seqs, n_heads, GLA_DK, GLA_DV), lambda b, h, s: (b, h, 0, 0))
    in_specs, args = [], []
    if meta is not None:
        pm, gm = meta
        in_specs += [pl.BlockSpec((N_META, wk), lambda b, h, s: (0, COL_KA // wk + h)),
                     pl.BlockSpec((N_META, wv), lambda b, h, s: (0, COL_VA // wv + h)),
                     pl.BlockSpec((N_META, wk), lambda b, h, s: (0, h))]
        args += [pm, pm, gm]
    else:
        in_specs.append(state_spec)
        args.append(state0)
    in_specs += [rows(wk, COL_QA), rows(wk, COL_KA), rows(wv, COL_VA), rows(wk, 0), rows(wv, COL_RA),
                 _const_spec((1, GLA_DV))]
    args += [proj, proj, proj, glog, proj, gla_norm_g]
    m = n_seq * seq_len
    return pl.pallas_call(
        functools.partial(_gla_kernel, n_seqs=n_seqs, n_heads=n_heads, chunk=chunk, n_chunks=n_chunks,
                          lead=N_META if meta is not None else 0),
        grid=(n_seq // n_seqs, groups, steps),
        in_specs=in_specs,
        out_specs=[rows(wv, 0), state_spec],
        out_shape=[jax.ShapeDtypeStruct((m, GLA_HEADS * GLA_DV), BF16),
                   jax.ShapeDtypeStruct((n_seq, GLA_HEADS, GLA_DK, GLA_DV), F32)],
        scratch_shapes=[pltpu.VMEM((n_seqs, n_heads, GLA_DK, GLA_DV), F32)],
        compiler_params=pltpu.CompilerParams(
            dimension_semantics=("arbitrary", "arbitrary", "arbitrary"), vmem_limit_bytes=VMEM_LIMIT),
    )(*args)


def _sb_logs(zb):
    nl = jnp.maximum(zb, 0.0) + jnp.log(1.0 + jnp.exp2(jnp.abs(zb) * (-LOG2E)))
    return zb - nl, nl


def _suffix_matrix(t):
    rows = lax.broadcasted_iota(jnp.int32, (t, t), 0)
    cols = lax.broadcasted_iota(jnp.int32, (t, t), 1)
    return (rows > cols).astype(BF16)


def _sb_suffix(nl, suffix):
    later = _dot(nl.astype(BF16), suffix)
    return later, jnp.sum(nl, axis=1, keepdims=True)


def _own_head_mask(n_rows):
    width = SB_HEADS * SB_HEAD_DIM
    row_id = lax.broadcasted_iota(jnp.int32, (n_rows, width), 0)
    col_id = lax.broadcasted_iota(jnp.int32, (n_rows, width), 1)
    return lax.shift_right_logical(col_id, SB_HEAD_DIM.bit_length() - 1) == (row_id & (SB_HEADS - 1))


def _sb_kernel(vq_ref, vt_ref, pt_ref, bias_ref, q_ref, k_ref, v_ref, km_ref, vm_ref,
               sbias_ref, sq_ref, skn_ref, svn_ref, ck_hbm, cv_hbm,
               o_ref, so_ref,
               q_scr, k_scr, vv_scr, bt_scr, sfx_scr, z_scr, w_scr, later_scr, tot_scr, nrun_scr, acc_scr,
               sqbd_scr, sacc_scr, srun_scr, spage_sfx_scr, kbuf, vbuf, sems,
               *, tile, n_main, n_visits, n_tok, seqs_per_step, n_pages):
    pair = pl.program_id(1)
    low_lanes = lax.broadcasted_iota(jnp.int32, (1, LANES), 1) < SB_HEAD_DIM
    both = lambda x: jnp.concatenate([jnp.where(low_lanes, x, 0.0), jnp.where(low_lanes, 0.0, x)], axis=0)
    DIAG, FULL, META, NONE = range(4)

    for j in range(n_main):
        sl = slice(j * tile, (j + 1) * tile)
        q_scr[j] = both(q_ref[sl, :] * (SB_HEAD_DIM ** -0.5)).astype(BF16)
        k_scr[j] = k_ref[sl, :].astype(BF16)
        vv_scr[j] = both(v_ref[sl, :]).astype(BF16)
    pad = jnp.zeros((tile - LANES, LANES), F32)
    k_scr[n_main] = jnp.concatenate([km_ref[...], pad], axis=0).astype(BF16)
    vv_scr[n_main] = both(jnp.concatenate([vm_ref[...], pad], axis=0)).astype(BF16)
    rows = lax.broadcasted_iota(jnp.int32, (tile, tile), 0)
    cols = lax.broadcasted_iota(jnp.int32, (tile, tile), 1)
    kinds = {DIAG: jnp.where(cols < rows, 0.0, NEG_BIG), FULL: jnp.zeros((tile, tile), F32),
             META: jnp.where(cols < N_META, 0.0, NEG_BIG), NONE: jnp.full((tile, tile), NEG_BIG, F32)}
    for kind, mask in kinds.items():
        for h in range(HEADS_PER_SLAB):
            bt_scr[kind, h * tile:(h + 1) * tile, :] = mask + bias_ref[HEADS_PER_SLAB * pair + h]
    sfx_scr[...] = _suffix_matrix(tile)

    def key_slot(q, t):
        return jnp.where(t <= q, q - t, n_main)

    def kind_of(q, t):
        return jnp.where(t == 0, DIAG, jnp.where(t <= q, FULL, jnp.where(t == q + 1, META, NONE)))

    def logits(v):
        q, t = vq_ref[v], vt_ref[v]
        return _dot_nt(q_scr[q], k_scr[key_slot(q, t)])

    def stage(v, cur, nxt):
        z_scr[nxt] = logits(v + 1)
        w, nl = _sb_logs(z_scr[cur] + bt_scr[kind_of(vq_ref[v], vt_ref[v])])
        later, tot = _sb_suffix(nl, sfx_scr[...])
        prev = jnp.maximum(v - 1, 0)
        q, t = vq_ref[prev], vt_ref[prev]
        fresh = t == 0
        nrun = jnp.where(fresh, 0.0, nrun_scr[...])
        a = jnp.exp(w_scr[nxt] - later_scr[nxt] - jnp.concatenate([nrun, nrun], axis=1)).astype(BF16)
        acc = jnp.where(fresh, 0.0, acc_scr[...]) + _dot(jnp.concatenate([a[:tile], a[tile:]], axis=1),
                                                          vv_scr[key_slot(q, t)])
        acc_scr[...] = acc
        o_ref[pl.ds(pl.multiple_of(q * tile, tile), tile), :] = acc.astype(BF16)
        nrun_scr[...] = nrun + tot_scr[nxt]
        w_scr[cur] = w
        later_scr[cur] = later
        tot_scr[cur] = jnp.broadcast_to(tot, tot_scr.shape[1:])

    z_scr[0] = logits(0)
    w_scr[1] = jnp.full(w_scr.shape[1:], NEG_BIG, F32)
    later_scr[1] = jnp.zeros(later_scr.shape[1:], F32)
    tot_scr[1] = jnp.zeros(tot_scr.shape[1:], F32)
    nrun_scr[...] = jnp.zeros(nrun_scr.shape, F32)
    acc_scr[...] = jnp.zeros(acc_scr.shape, F32)

    n_rows = n_tok * SB_HEADS
    width = SB_HEADS * SB_HEAD_DIM
    own_head = _own_head_mask(n_rows)
    sbias = sbias_ref[...]
    tok_of_row = lax.shift_right_logical(lax.broadcasted_iota(jnp.int32, (n_rows, 1), 0),
                                         SB_HEADS.bit_length() - 1)
    for s in range(seqs_per_step):
        r0 = s * n_tok
        q_rows = jnp.concatenate(
            [jnp.broadcast_to(sq_ref[r0 + t:r0 + t + 1, :] * (SB_HEAD_DIM ** -0.5), (SB_HEADS, width))
             for t in range(n_tok)], axis=0)
        qbd = jnp.where(own_head, q_rows, 0.0).astype(BF16)
        sqbd_scr[s] = qbd
        qf = qbd.astype(F32)
        nrun = jnp.zeros((n_rows, 1), F32)
        acc = jnp.zeros((n_rows, width), F32)
        for j in reversed(range(n_tok)):
            kj = skn_ref[r0 + j:r0 + j + 1, :].astype(BF16).astype(F32)
            vj = svn_ref[r0 + j:r0 + j + 1, :].astype(BF16).astype(F32)
            z = jnp.sum(qf * kj, axis=-1, keepdims=True) + sbias
            w, nl = _sb_logs(jnp.where(tok_of_row > j, z, NEG_BIG))
            acc = acc + jnp.exp(w - nrun) * vj
            nrun = nrun + nl
        sacc_scr[s] = acc
        srun_scr[s] = nrun
    spage_sfx_scr[...] = _suffix_matrix(kbuf.shape[2])

    trips = pl.cdiv(n_visits + 1, SB_VISITS_PER_TRIP)
    per_trip = SB_PAGES_PER_TRIP
    n_total = seqs_per_step * n_pages
    ring = kbuf.shape[0]
    seq0 = (pl.program_id(0) * pl.num_programs(1) + pair) * seqs_per_step

    def page_copies(g):
        g = jnp.asarray(g, jnp.int32)
        gc = jnp.minimum(g, n_total - 1)
        pool = pt_ref[seq0 + lax.div(gc, jnp.int32(n_pages)), n_pages - 1 - lax.rem(gc, jnp.int32(n_pages))]
        slot = lax.rem(g, jnp.int32(ring))
        return (pltpu.make_async_copy(ck_hbm.at[pool], kbuf.at[slot], sems.at[0, slot]),
                pltpu.make_async_copy(cv_hbm.at[pool], vbuf.at[slot], sems.at[1, slot]))

    def start_pages(g0):
        for i in range(per_trip):
            for copy in page_copies(g0 + i):
                copy.start()

    def wait_pages(g0):
        for i in range(per_trip):
            for copy in page_copies(g0 + i):
                copy.wait()

    def page_logits(g0):
        seq = lax.div(jnp.minimum(g0, n_total - 1), jnp.int32(n_pages))
        off = sbias + jnp.where(g0 < n_total, 0.0, NEG_BIG)
        return [_dot(sqbd_scr[seq], kbuf[lax.rem(g0 + i, jnp.int32(ring))].astype(BF16)) + off
                for i in range(SB_PAGE_GROUP)]

    def page_suffix(zs):
        logs = [_sb_logs(z) for z in zs]
        sums = [_sb_suffix(nl, spage_sfx_scr[...]) for _, nl in logs]
        return [(w, later, tot) for (w, _), (later, tot) in zip(logs, sums)]

    def page_finish(g0, parts):
        seq = lax.div(jnp.minimum(g0, n_total - 1), jnp.int32(n_pages))
        nrun = srun_scr[seq]
        acc = sacc_scr[seq]
        for i, (w, later, tot) in enumerate(parts):
            a = jnp.exp(w - later - nrun).astype(BF16)
            acc = acc + _dot_nt(a, vbuf[lax.rem(g0 + i, jnp.int32(ring))].astype(BF16))
            nrun = nrun + tot
        sacc_scr[seq] = acc
        srun_scr[seq] = nrun

    for ahead in range(SB_PREFETCH_TRIPS):
        start_pages(ahead * per_trip)

    groups = per_trip // SB_PAGE_GROUP
    visits_per_group = SB_VISITS_PER_TRIP // groups

    def body(u, carry):
        wait_pages(u * per_trip)
        start_pages((u + SB_PREFETCH_TRIPS) * per_trip)
        for k in range(groups):
            v0 = SB_VISITS_PER_TRIP * u + k * visits_per_group
            g0 = u * per_trip + k * SB_PAGE_GROUP
            zs = page_logits(g0)
            stage(v0, 0, 1)
            parts = page_suffix(zs)
            stage(v0 + 1, 1, 0)
            page_finish(g0, parts)
            for i in range(2, visits_per_group):
                stage(v0 + i, i % 2, (i + 1) % 2)
        return carry

    lax.fori_loop(0, trips, body, 0)
    for ahead in range(SB_PREFETCH_TRIPS):
        wait_pages((trips + ahead) * per_trip)

    for s in range(seqs_per_step):
        picked = jnp.where(own_head, sacc_scr[s], 0.0)
        for t in range(n_tok):
            so_ref[s * n_tok + t:s * n_tok + t + 1, :] = jnp.sum(
                picked[t * SB_HEADS:(t + 1) * SB_HEADS, :], axis=0, keepdims=True)


def _sb_attention(proj, proj_meta, proj_s, cache_k, cache_v, page_table, sb_bias, *, n_seq, seq_len, tile,
                  n_dec, n_tok):
    p3 = proj.reshape(n_seq, seq_len, PROJ_W)
    slab = lambda col: col // LANES
    n_main = seq_len // tile
    pair_rows = HEADS_PER_SLAB * tile
    n_pairs = SB_HEADS // HEADS_PER_SLAB
    width = SB_HEADS * SB_HEAD_DIM
    n_pages = page_table.shape[1]
    page_size = cache_k.shape[1]
    seqs_per_step = n_dec // (n_seq * n_pairs)
    assert seqs_per_step * n_seq * n_pairs == n_dec and n_pages % SB_PAGES_PER_TRIP == 0
    step_rows = seqs_per_step * n_tok
    n_rows = n_tok * SB_HEADS
    ck = cache_k.transpose(0, 2, 3, 1).reshape(cache_k.shape[0], width, page_size)
    cv = cache_v.transpose(0, 2, 3, 1).reshape(cache_v.shape[0], width, page_size)
    bias_rows = jnp.tile(sb_bias, n_tok).reshape(n_rows, 1)
    ring = (SB_PREFETCH_TRIPS + 1) * SB_PAGES_PER_TRIP
    visits = [(q, t) for q in range(n_main) for t in range(q + 2)]
    n_visits = len(visits)
    visits += [(n_main - 1, n_main + 1)] * (SB_VISITS_PER_TRIP + 2)
    vq = jnp.asarray([q for q, _ in visits], jnp.int32)
    vt = jnp.asarray([t for _, t in visits], jnp.int32)
    assert seqs_per_step * n_pages <= pl.cdiv(n_visits + 1, SB_VISITS_PER_TRIP) * SB_PAGES_PER_TRIP
    seq_blk = lambda col: pl.BlockSpec((None, seq_len, LANES), lambda b, p, *_: (b, 0, slab(col) + p))
    meta_blk = lambda col: pl.BlockSpec((LANES, LANES), lambda b, p, *_: (0, slab(col) + p))
    new_blk = lambda col: pl.BlockSpec((step_rows, width), lambda b, p, *_: (b * n_pairs + p, col // width))
    grid_spec = pltpu.PrefetchScalarGridSpec(
        num_scalar_prefetch=3,
        grid=(n_seq, n_pairs),
        in_specs=[pl.BlockSpec(memory_space=pltpu.SMEM),
                  seq_blk(COL_QB), seq_blk(COL_KB), seq_blk(COL_VB), meta_blk(COL_KB), meta_blk(COL_VB),
                  pl.BlockSpec((n_rows, 1), lambda b, p, *_: (0, 0)),
                  new_blk(COL_QB), new_blk(COL_KB), new_blk(COL_VB),
                  pl.BlockSpec(memory_space=pl.ANY), pl.BlockSpec(memory_space=pl.ANY)],
        out_specs=[pl.BlockSpec((None, seq_len, LANES), lambda b, p, *_: (b, 0, p)),
                   pl.BlockSpec((step_rows, width), lambda b, p, *_: (b * n_pairs + p, 0))],
        scratch_shapes=[pltpu.VMEM((n_main, pair_rows, LANES), BF16),
                        pltpu.VMEM((n_main + 1, tile, LANES), BF16),
                        pltpu.VMEM((n_main + 1, pair_rows, LANES), BF16),
                        pltpu.VMEM((4, pair_rows, tile), F32),
                        pltpu.VMEM((tile, tile), BF16),
                        pltpu.VMEM((2, pair_rows, tile), F32),
                        pltpu.VMEM((2, pair_rows, tile), F32),
                        pltpu.VMEM((2, pair_rows, tile), F32),
                        pltpu.VMEM((2, pair_rows, LANES), F32),
                        pltpu.VMEM((pair_rows, LANES), F32),
                        pltpu.VMEM((tile, LANES), F32),
                        pltpu.VMEM((seqs_per_step, n_rows, width), BF16),
                        pltpu.VMEM((seqs_per_step, n_rows, width), F32),
                        pltpu.VMEM((seqs_per_step, n_rows, 1), F32),
                        pltpu.VMEM((page_size, page_size), BF16),
                        pltpu.VMEM((ring, width, page_size), F32),
                        pltpu.VMEM((ring, width, page_size), F32),
                        pltpu.SemaphoreType.DMA((2, ring))],
    )
    out_p, out_s = pl.pallas_call(
        functools.partial(_sb_kernel, tile=tile, n_main=n_main, n_visits=n_visits, n_tok=n_tok,
                          seqs_per_step=seqs_per_step, n_pages=n_pages),
        grid_spec=grid_spec,
        out_shape=[jax.ShapeDtypeStruct((n_seq, seq_len, width), BF16),
                   jax.ShapeDtypeStruct((n_dec * n_tok, width), F32)],
        compiler_params=pltpu.CompilerParams(
            dimension_semantics=("arbitrary", "arbitrary"), vmem_limit_bytes=VMEM_LIMIT),
    )(vq, vt, page_table, sb_bias, p3, p3, p3, proj_meta, proj_meta, bias_rows, proj_s, proj_s, proj_s, ck, cv)
    return out_p.reshape(n_seq * seq_len, width), out_s


def _merge_kernel(oa_ref, ob_ref, ga_ref, gb_ref, h_ref, woa_ref, wob_ref, wout_ref, g_ref, out_ref):
    pa = _dot(oa_ref[...].astype(BF16), woa_ref[...])
    pb = _dot(ob_ref[...].astype(BF16), wob_ref[...])
    mixed = jax.nn.sigmoid(ga_ref[...]) * pa + jax.nn.sigmoid(gb_ref[...]) * pb
    out_ref[...] = h_ref[...] + _rms(_dot(mixed.astype(BF16), wout_ref[...]), g_ref[...])


def _merge(o_a, o_b, proj, h, w_o_gla, w_o_sb, w_out, g_post, *, tm):
    m, d = h.shape
    row = pl.BlockSpec((tm, d), lambda i: (i, 0))
    return pl.pallas_call(
        _merge_kernel,
        grid=(m // tm,),
        in_specs=[row, row,
                  pl.BlockSpec((tm, d), lambda i: (i, COL_GA // d)),
                  pl.BlockSpec((tm, d), lambda i: (i, COL_GB // d)),
                  row, _const_spec(w_o_gla.shape), _const_spec(w_o_sb.shape), _const_spec(w_out.shape),
                  _const_spec((1, d))],
        out_specs=row,
        out_shape=jax.ShapeDtypeStruct((m, d), F32),
        compiler_params=pltpu.CompilerParams(
            dimension_semantics=("arbitrary",), vmem_limit_bytes=VMEM_LIMIT),
    )(o_a, o_b, proj, proj, h, w_o_gla, w_o_sb, w_out, g_post)


def _row_tile(m, cap):
    return m if m <= cap else cap


def kernel(x_prompt, x_sample, cache_k, cache_v, state_gla, page_table, meta_tokens,
           ffn1_pre_g, ffn1_w_gate, ffn1_w_up, ffn1_w_down, ffn1_post_g,
           mix_pre_g, w_in, w_gk2, b_gk, gla_norm_g, sb_bias, w_o_gla, w_o_sb, w_out, mix_post_g,
           ffn2_pre_g, ffn2_w_gate, ffn2_w_up, ffn2_w_down, ffn2_post_g):
    n_seq, seq_len, d = x_prompt.shape
    n_dec, n_tok, _ = x_sample.shape
    assert ffn1_pre_g.shape[0] == 1, "single layer"
    qk = GLA_HEADS * GLA_DK
    v_w = GLA_HEADS * GLA_DV
    rank = w_gk2.shape[1]

    bf = lambda w: w[0].astype(BF16)
    w1g, w1u, w1d = bf(ffn1_w_gate), bf(ffn1_w_up), bf(ffn1_w_down)
    w2g, w2u, w2d = bf(ffn2_w_gate), bf(ffn2_w_up), bf(ffn2_w_down)
    w_in0 = w_in[0]
    gk0 = 2 * qk + v_w
    w_main = jnp.concatenate([w_in0[:, :gk0], w_in0[:, gk0 + rank:]], axis=1).astype(BF16)
    w_gk1 = jnp.pad(w_in0[:, gk0:gk0 + rank], ((0, 0), (0, LANES - rank))).astype(BF16)
    w_gk2p = jnp.pad(w_gk2[0], ((0, LANES - rank), (0, 0))).astype(BF16)
    woa, wob, wout = bf(w_o_gla), bf(w_o_sb), bf(w_out)
    row = lambda g: g[0].reshape(1, -1)
    bias = sb_bias[0]

    def pre_mix(x, tm):
        h, u = _ffn(x, row(ffn1_pre_g), w1g, w1u, w1d, row(ffn1_post_g), row(mix_pre_g),
                    tm=_row_tile(x.shape[0], FFN_ROW_GROUPS * tm))
        proj, glog = _proj(u, w_main, w_gk1, w_gk2p, row(b_gk), tm=_row_tile(x.shape[0], 2048), tn=1024)
        return h, u, proj, glog

    def post_mix(h, proj, o_a, o_b, tm):
        h2 = _merge(o_a, o_b, proj, h, woa, wob, wout, row(mix_post_g), tm=tm)
        return _ffn(h2, row(ffn2_pre_g), w2g, w2u, w2d, row(ffn2_post_g),
                    tm=_row_tile(h.shape[0], FFN_ROW_GROUPS * tm))

    x_meta = jnp.pad(meta_tokens.astype(F32), ((0, LANES - N_META), (0, 0)))
    _, u_m, proj_m, glog_m = pre_mix(x_meta, LANES)

    m_p = n_seq * seq_len
    tm_p = _row_tile(m_p, 512)
    h_p, u_p, proj_p, glog_p = pre_mix(x_prompt.reshape(m_p, d), tm_p)
    oa_p, s_p = _gla(proj_p, glog_p, row(gla_norm_g), n_seq=n_seq, seq_len=seq_len,
                     rows_per_step=_row_tile(seq_len, 512), chunk=GLA_CHUNK, n_heads=4,
                     meta=(proj_m, glog_m))

    m_s = n_dec * n_tok
    tm_s = _row_tile(m_s, 512)
    h_s, _, proj_s, glog_s = pre_mix(x_sample.reshape(m_s, d), tm_s)

    ob_p, ob_s = _sb_attention(proj_p, proj_m, proj_s, cache_k[0], cache_v[0], page_table, bias,
                               n_seq=n_seq, seq_len=seq_len, tile=MXU_DIM, n_dec=n_dec, n_tok=n_tok)
    y_p = post_mix(h_p, proj_p, oa_p, ob_p, tm_p)

    pad_tok = lambda a: jnp.pad(a.reshape(n_dec, n_tok, -1), ((0, 0), (0, SUBLANES - n_tok), (0, 0))
                                ).reshape(n_dec * SUBLANES, -1)
    oa_s, s_s = _gla(pad_tok(proj_s[:, :COL_QB]), pad_tok(glog_s), row(gla_norm_g), n_seq=n_dec,
                     seq_len=SUBLANES, rows_per_step=SUBLANES, chunk=SUBLANES, n_heads=GLA_HEADS,
                     n_seqs=4, state0=state_gla[0])
    oa_s = oa_s.reshape(n_dec, SUBLANES, v_w)[:, :n_tok].reshape(m_s, v_w)
    y_s = post_mix(h_s, proj_s, oa_s, ob_s, tm_s)

    heads = (SB_HEADS, SB_HEAD_DIM)
    w_sb = SB_HEADS * SB_HEAD_DIM

    u_full = jnp.concatenate([jnp.broadcast_to(u_m[:N_META][None], (n_seq, N_META, d)),
                              u_p.reshape(n_seq, seq_len, d)], axis=1)
    k_t, v_t = _kv_t(w_main[:, COL_KB:COL_KB + w_sb].T, w_main[:, COL_VB:COL_VB + w_sb].T, u_full,
                     tp=512)
    kv_out = lambda a: a.reshape(n_seq, *heads, N_META + seq_len).transpose(0, 3, 1, 2)[None]

    return (y_p.reshape(n_seq, seq_len, d),
            y_s.reshape(n_dec, n_tok, d),
            kv_out(k_t), kv_out(v_t),
            s_p[None],
            proj_s[:, COL_KB:COL_KB + w_sb].reshape(1, n_dec, n_tok, *heads),
            proj_s[:, COL_VB:COL_VB + w_sb].reshape(1, n_dec, n_tok, *heads),
            s_s[None])
```

```python
import functools

import jax
import jax.numpy as jnp
from jax import lax
from jax.experimental import pallas as pl
from jax.experimental.pallas import tpu as pltpu

F32 = jnp.float32
BF16 = jnp.bfloat16

EPS = 1e-6
N_META = 16
GLA_HEADS = 4
GLA_DK = 128
GLA_DV = 256
GLA_TAU = 16.0
GLA_CHUNK = 64
SB_HEADS = 16
SB_HEAD_DIM = 64
LANES = 128
SUBLANES = 8
MXU_DIM = 256
HEADS_PER_SLAB = LANES // SB_HEAD_DIM
VMEM_LIMIT = 56 * 1024 * 1024
NEG_BIG = -1e30
LOG2E = 1.4426950408889634
FFN_ROW_GROUPS = 2
SB_VISITS_PER_TRIP = 4
SB_PAGES_PER_TRIP = 2
SB_PAGE_GROUP = 2
SB_PREFETCH_TRIPS = 4

COL_QA, COL_KA, COL_VA, COL_RA = 0, 512, 1024, 2048
COL_QB, COL_KB, COL_VB, COL_GA, COL_GB = 3072, 4096, 5120, 6144, 7168
PROJ_W = 8192


def _rms(x, g):
    return x * lax.rsqrt(jnp.mean(x * x, axis=-1, keepdims=True) + EPS) * g


def _const_spec(shape):
    return pl.BlockSpec(shape, lambda *_: (0,) * len(shape), pipeline_mode=pl.Buffered(1))


def _dot(a, b):
    return jnp.dot(a, b, preferred_element_type=F32)


def _dot_nt(a, b):
    return lax.dot_general(a, b, (((1,), (1,)), ((), ())), preferred_element_type=F32)


def _dot_tn(a, b):
    return lax.dot_general(a, b, (((0,), (0,)), ((), ())), preferred_element_type=F32)


def _split3(x):
    hi = x.astype(BF16)
    r1 = x - hi.astype(F32)
    mid = r1.astype(BF16)
    lo = (r1 - mid.astype(F32)).astype(BF16)
    return hi, mid, lo


def _ffn_kernel(x_ref, gpre_ref, wg_ref, wu_ref, wd_ref, gpost_ref, *rest, ff_chunk, emit_u):
    if emit_u:
        gnext_ref, h_ref, u_ref, act_ref = rest
    else:
        h_ref, act_ref = rest
    tm = x_ref.shape[0]
    n_sub = FFN_ROW_GROUPS if tm % (FFN_ROW_GROUPS * SUBLANES) == 0 else 1
    subs = [slice(s * tm // n_sub, (s + 1) * tm // n_sub) for s in range(n_sub)]
    xs = [x_ref[rs, :] for rs in subs]
    us = [_rms(x, gpre_ref[...]).astype(BF16) for x in xs]
    d_ff = wg_ref.shape[1]
    for c in range(d_ff // ff_chunk):
        sl = slice(c * ff_chunk, (c + 1) * ff_chunk)
        for rs, u in zip(subs, us):
            gate = _dot(u, wg_ref[:, sl])
            up = _dot(u, wu_ref[:, sl])
            act_ref[rs, sl] = (gate * jax.nn.sigmoid(gate) * up).astype(BF16)
    ys = [_dot(act_ref[rs, :], wd_ref[...]) for rs in subs]
    for rs, x, y in zip(subs, xs, ys):
        h = x + 0.5 * _rms(y, gpost_ref[...])
        h_ref[rs, :] = h
        if emit_u:
            u_ref[rs, :] = _rms(h, gnext_ref[...]).astype(BF16)


def _ffn(x, g_pre, wg, wu, wd, g_post, g_next=None, *, tm):
    m, d = x.shape
    d_ff = wg.shape[1]
    emit_u = g_next is not None
    row = pl.BlockSpec((tm, d), lambda i: (i, 0))
    in_specs = [row, _const_spec((1, d)), _const_spec((d, d_ff)), _const_spec((d, d_ff)),
                _const_spec((d_ff, d)), _const_spec((1, d))]
    args = [x, g_pre, wg, wu, wd, g_post]
    out_shape = [jax.ShapeDtypeStruct((m, d), F32)]
    out_specs = [row]
    if emit_u:
        in_specs.append(_const_spec((1, d)))
        args.append(g_next)
        out_shape.append(jax.ShapeDtypeStruct((m, d), BF16))
        out_specs.append(row)
    outs = pl.pallas_call(
        functools.partial(_ffn_kernel, ff_chunk=MXU_DIM, emit_u=emit_u),
        grid=(m // tm,),
        in_specs=in_specs,
        out_specs=out_specs,
        out_shape=out_shape,
        scratch_shapes=[pltpu.VMEM((tm, d_ff), BF16)],
        compiler_params=pltpu.CompilerParams(
            dimension_semantics=("arbitrary",), vmem_limit_bytes=VMEM_LIMIT),
    )(*args)
    return outs if emit_u else outs[0]


def _proj_kernel(u_ref, w_ref, wgk1_ref, wgk2_ref, bgk_ref, p_ref, g_ref):
    u = u_ref[...]
    p_ref[...] = _dot(u, w_ref[...])

    @pl.when(pl.program_id(1) == 0)
    def _():
        low = _dot(u, wgk1_ref[...]).astype(BF16)
        zg = _dot(low, wgk2_ref[...]) + bgk_ref[...]
        g_ref[...] = (jnp.minimum(zg, 0.0) - jnp.log(1.0 + jnp.exp(-jnp.abs(zg)))) * (1.0 / GLA_TAU)


def _proj(u, w_main, w_gk1, w_gk2, b_gk, *, tm, tn):
    m, d = u.shape
    n = w_main.shape[1]
    qk = w_gk2.shape[1]
    return pl.pallas_call(
        _proj_kernel,
        grid=(m // tm, n // tn),
        in_specs=[pl.BlockSpec((tm, d), lambda i, j: (i, 0)),
                  pl.BlockSpec((d, tn), lambda i, j: (0, j)),
                  _const_spec(w_gk1.shape), _const_spec(w_gk2.shape), _const_spec((1, qk))],
        out_specs=[pl.BlockSpec((tm, tn), lambda i, j: (i, j)),
                   pl.BlockSpec((tm, qk), lambda i, j: (i, 0))],
        out_shape=[jax.ShapeDtypeStruct((m, n), F32), jax.ShapeDtypeStruct((m, qk), F32)],
        compiler_params=pltpu.CompilerParams(
            dimension_semantics=("arbitrary", "arbitrary"), vmem_limit_bytes=VMEM_LIMIT),
    )(u, w_main, w_gk1, w_gk2, b_gk)


def _kv_t_kernel(wk_ref, wv_ref, u_ref, k_ref, v_ref):
    u = u_ref[...]
    k_ref[...] = _dot_nt(wk_ref[...], u)
    v_ref[...] = _dot_nt(wv_ref[...], u)


def _kv_t(wk_t, wv_t, u_full, *, tp):
    width, d = wk_t.shape
    n_seq, n_pos, _ = u_full.shape
    out = pl.BlockSpec((None, width, tp), lambda b, j: (b, 0, j))
    return pl.pallas_call(
        _kv_t_kernel,
        grid=(n_seq, pl.cdiv(n_pos, tp)),
        in_specs=[_const_spec((width, d)), _const_spec((width, d)),
                  pl.BlockSpec((None, tp, d), lambda b, j: (b, j, 0))],
        out_specs=[out, out],
        out_shape=[jax.ShapeDtypeStruct((n_seq, width, n_pos), F32)] * 2,
        compiler_params=pltpu.CompilerParams(
            dimension_semantics=("arbitrary", "arbitrary"), vmem_limit_bytes=VMEM_LIMIT),
    )(wk_t, wv_t, u_full)


def _gla_local(problems, c):
    rows = lax.broadcasted_iota(jnp.int32, (c, c), 0)
    cols = lax.broadcasted_iota(jnp.int32, (c, c), 1)
    causal = cols <= rows
    tri = causal.astype(BF16)
    ones_c = jnp.ones((c, LANES), BF16)
    parts = [_split3(g) for _, _, _, g in problems]
    cums = [_dot(tri, p[0]) + _dot(tri, p[1]) + _dot(tri, p[2]) for p in parts]
    sums = [_dot_tn(p[0], ones_c) + _dot_tn(p[1], ones_c) + _dot_tn(p[2], ones_c) for p in parts]
    q_ins, k_ins, k_ends, vbs = [], [], [], []
    for (q, k, v, _), b in zip(problems, cums):
        q_ins.append((q * (GLA_DK ** -0.5) * jnp.exp(b)).astype(BF16))
        k_ins.append((k * jnp.exp(-b)).astype(BF16))
        k_ends.append((k * jnp.exp(b[c - 1:c, :] - b)).astype(BF16))
        vbs.append(v.astype(BF16))
    scores = [_dot_nt(qi, ki) for qi, ki in zip(q_ins, k_ins)]
    o_intra = [_dot(jnp.where(causal, a, 0.0).astype(BF16), vb) for a, vb in zip(scores, vbs)]
    d_state = [_dot_tn(ke, vb) for ke, vb in zip(k_ends, vbs)]
    decays = [jnp.concatenate([jnp.exp(bs)] * (GLA_DV // LANES), axis=1) for bs in sums]
    return list(zip(q_ins, o_intra, d_state, decays))


def _gla_kernel(*refs, n_seqs, n_heads, chunk, n_chunks, lead):
    if lead:
        km_ref, vm_ref, gm_ref = refs[:3]
        refs = refs[3:]
    else:
        s0_ref = refs[0]
        refs = refs[1:]
    q_ref, k_ref, v_ref, g_ref, r_ref, gn_ref, o_ref, sfin_ref, s_scr = refs
    step = pl.program_id(2)
    dk = lambda h: slice(h * GLA_DK, (h + 1) * GLA_DK)
    dv = lambda h: slice(h * GLA_DV, (h + 1) * GLA_DV)
    rows = lambda sq, ci: slice((sq * n_chunks + ci) * chunk, (sq * n_chunks + ci + 1) * chunk)
    chains = [(sq, h) for sq in range(n_seqs) for h in range(n_heads)]

    @pl.when(step == 0)
    def _():
        if lead:
            zeros = jnp.zeros((lead, GLA_DK), F32)
            metas = _gla_local([(zeros, km_ref[:, dk(h)], vm_ref[:, dv(h)], gm_ref[:, dk(h)])
                                for h in range(n_heads)], lead)
            for h in range(n_heads):
                s_scr[0, h] = metas[h][2]
        else:
            s_scr[...] = s0_ref[...]

    problems = []
    for sq, h in chains:
        for ci in range(n_chunks):
            sl = rows(sq, ci)
            problems.append((q_ref[sl, dk(h)], k_ref[sl, dk(h)], v_ref[sl, dv(h)], g_ref[sl, dk(h)]))
    local = _gla_local(problems, chunk)
    gn = gn_ref[...]
    for c, (sq, h) in enumerate(chains):
        s = s_scr[sq, h]
        for ci in range(n_chunks):
            sl = rows(sq, ci)
            q_in, o_intra, d_state, decay = local[c * n_chunks + ci]
            o = o_intra + _dot(q_in, s.astype(BF16))
            s = decay * s + d_state
            r = r_ref[sl, dv(h)]
            o_ref[sl, dv(h)] = (_rms(o, gn) * (r * jax.nn.sigmoid(r))).astype(BF16)
        s_scr[sq, h] = s

    @pl.when(step == pl.num_programs(2) - 1)
    def _():
        sfin_ref[...] = s_scr[...]


def _gla(proj, glog, gla_norm_g, *, n_seq, seq_len, rows_per_step, chunk, n_heads, n_seqs=1, meta=None,
         state0=None):
    steps = seq_len // rows_per_step
    n_chunks = rows_per_step // chunk
    groups = GLA_HEADS // n_heads
    wk, wv = n_heads * GLA_DK, n_heads * GLA_DV
    assert n_seqs == 1 or (steps == 1 and meta is None)
    assert n_seq % n_seqs == 0

    def rows(width, col0):
        return pl.BlockSpec((n_seqs * rows_per_step, width),
                            lambda b, h, s: (b * steps + s, col0 // width + h))

    state_spec = pl.BlockSpec((n_seqs, n_heads, GLA_DK, GLA_DV), lambda b, h, s: (b, h, 0, 0))
    in_specs, args = [], []
    if meta is not None:
        pm, gm = meta
        in_specs += [pl.BlockSpec((N_META, wk), lambda b, h, s: (0, COL_KA // wk + h)),
                     pl.BlockSpec((N_META, wv), lambda b, h, s: (0, COL_VA // wv + h)),
                     pl.BlockSpec((N_META, wk), lambda b, h, s: (0, h))]
        args += [pm, pm, gm]
    else:
        in_specs.append(state_spec)
        args.append(state0)
    in_specs += [rows(wk, COL_QA), rows(wk, COL_KA), rows(wv, COL_VA), rows(wk, 0), rows(wv, COL_RA),
                 _const_spec((1, GLA_DV))]
    args += [proj, proj, proj, glog, proj, gla_norm_g]
    m = n_seq * seq_len
    return pl.pallas_call(
        functools.partial(_gla_kernel, n_seqs=n_seqs, n_heads=n_heads, chunk=chunk, n_chunks=n_chunks,
                          lead=N_META if meta is not None else 0),
        grid=(n_seq // n_seqs, groups, steps),
        in_specs=in_specs,
        out_specs=[rows(wv, 0), state_spec],
        out_shape=[jax.ShapeDtypeStruct((m, GLA_HEADS * GLA_DV), BF16),
                   jax.ShapeDtypeStruct((n_seq, GLA_HEADS, GLA_DK, GLA_DV), F32)],
        scratch_shapes=[pltpu.VMEM((n_seqs, n_heads, GLA_DK, GLA_DV), F32)],
        compiler_params=pltpu.CompilerParams(
            dimension_semantics=("arbitrary", "arbitrary", "arbitrary"), vmem_limit_bytes=VMEM_LIMIT),
    )(*args)


def _sb_logs(zb):
    nl = jnp.maximum(zb, 0.0) + jnp.log(1.0 + jnp.exp2(jnp.abs(zb) * (-LOG2E)))
    return zb - nl, nl


def _suffix_matrix(t):
    rows = lax.broadcasted_iota(jnp.int32, (t, t), 0)
    cols = lax.broadcasted_iota(jnp.int32, (t, t), 1)
    return (rows > cols).astype(BF16)


def _sb_suffix(nl, suffix):
    later = _dot(nl.astype(BF16), suffix)
    return later, jnp.sum(nl, axis=1, keepdims=True)


def _own_head_mask(n_rows):
    width = SB_HEADS * SB_HEAD_DIM
    row_id = lax.broadcasted_iota(jnp.int32, (n_rows, width), 0)
    col_id = lax.broadcasted_iota(jnp.int32, (n_rows, width), 1)
    return lax.shift_right_logical(col_id, SB_HEAD_DIM.bit_length() - 1) == (row_id & (SB_HEADS - 1))


def _sb_kernel(vq_ref, vt_ref, pt_ref, bias_ref, q_ref, k_ref, v_ref, km_ref, vm_ref,
               sbias_ref, sq_ref, skn_ref, svn_ref, ck_hbm, cv_hbm,
               o_ref, so_ref,
               q_scr, k_scr, vv_scr, bt_scr, sfx_scr, z_scr, w_scr, later_scr, tot_scr, nrun_scr, acc_scr,
               sqbd_scr, sacc_scr, srun_scr, spage_sfx_scr, kbuf, vbuf, sems,
               *, tile, n_main, n_visits, n_tok, seqs_per_step, n_pages):
    pair = pl.program_id(1)
    low_lanes = lax.broadcasted_iota(jnp.int32, (1, LANES), 1) < SB_HEAD_DIM
    both = lambda x: jnp.concatenate([jnp.where(low_lanes, x, 0.0), jnp.where(low_lanes, 0.0, x)], axis=0)
    DIAG, FULL, META, NONE = range(4)

    for j in range(n_main):
        sl = slice(j * tile, (j + 1) * tile)
        q_scr[j] = both(q_ref[sl, :] * (SB_HEAD_DIM ** -0.5)).astype(BF16)
        k_scr[j] = k_ref[sl, :].astype(BF16)
        vv_scr[j] = both(v_ref[sl, :]).astype(BF16)
    pad = jnp.zeros((tile - LANES, LANES), F32)
    k_scr[n_main] = jnp.concatenate([km_ref[...], pad], axis=0).astype(BF16)
    vv_scr[n_main] = both(jnp.concatenate([vm_ref[...], pad], axis=0)).astype(BF16)
    rows = lax.broadcasted_iota(jnp.int32, (tile, tile), 0)
    cols = lax.broadcasted_iota(jnp.int32, (tile, tile), 1)
    kinds = {DIAG: jnp.where(cols < rows, 0.0, NEG_BIG), FULL: jnp.zeros((tile, tile), F32),
             META: jnp.where(cols < N_META, 0.0, NEG_BIG), NONE: jnp.full((tile, tile), NEG_BIG, F32)}
    for kind, mask in kinds.items():
        for h in range(HEADS_PER_SLAB):
            bt_scr[kind, h * tile:(h + 1) * tile, :] = mask + bias_ref[HEADS_PER_SLAB * pair + h]
    sfx_scr[...] = _suffix_matrix(tile)

    def key_slot(q, t):
        return jnp.where(t <= q, q - t, n_main)

    def kind_of(q, t):
        return jnp.where(t == 0, DIAG, jnp.where(t <= q, FULL, jnp.where(t == q + 1, META, NONE)))

    def logits(v):
        q, t = vq_ref[v], vt_ref[v]
        return _dot_nt(q_scr[q], k_scr[key_slot(q, t)])

    def stage(v, cur, nxt):
        z_scr[nxt] = logits(v + 1)
        w, nl = _sb_logs(z_scr[cur] + bt_scr[kind_of(vq_ref[v], vt_ref[v])])
        later, tot = _sb_suffix(nl, sfx_scr[...])
        prev = jnp.maximum(v - 1, 0)
        q, t = vq_ref[prev], vt_ref[prev]
        fresh = t == 0
        nrun = jnp.where(fresh, 0.0, nrun_scr[...])
        a = jnp.exp(w_scr[nxt] - later_scr[nxt] - jnp.concatenate([nrun, nrun], axis=1)).astype(BF16)
        acc = jnp.where(fresh, 0.0, acc_scr[...]) + _dot(jnp.concatenate([a[:tile], a[tile:]], axis=1),
                                                          vv_scr[key_slot(q, t)])
        acc_scr[...] = acc
        o_ref[pl.ds(pl.multiple_of(q * tile, tile), tile), :] = acc.astype(BF16)
        nrun_scr[...] = nrun + tot_scr[nxt]
        w_scr[cur] = w
        later_scr[cur] = later
        tot_scr[cur] = jnp.broadcast_to(tot, tot_scr.shape[1:])

    z_scr[0] = logits(0)
    w_scr[1] = jnp.full(w_scr.shape[1:], NEG_BIG, F32)
    later_scr[1] = jnp.zeros(later_scr.shape[1:], F32)
    tot_scr[1] = jnp.zeros(tot_scr.shape[1:], F32)
    nrun_scr[...] = jnp.zeros(nrun_scr.shape, F32)
    acc_scr[...] = jnp.zeros(acc_scr.shape, F32)

    n_rows = n_tok * SB_HEADS
    width = SB_HEADS * SB_HEAD_DIM
    own_head = _own_head_mask(n_rows)
    sbias = sbias_ref[...]
    tok_of_row = lax.shift_right_logical(lax.broadcasted_iota(jnp.int32, (n_rows, 1), 0),
                                         SB_HEADS.bit_length() - 1)
    for s in range(seqs_per_step):
        r0 = s * n_tok
        q_rows = jnp.concatenate(
            [jnp.broadcast_to(sq_ref[r0 + t:r0 + t + 1, :] * (SB_HEAD_DIM ** -0.5), (SB_HEADS, width))
             for t in range(n_tok)], axis=0)
        qbd = jnp.where(own_head, q_rows, 0.0).astype(BF16)
        sqbd_scr[s] = qbd
        qf = qbd.astype(F32)
        nrun = jnp.zeros((n_rows, 1), F32)
        acc = jnp.zeros((n_rows, width), F32)
        for j in reversed(range(n_tok)):
            kj = skn_ref[r0 + j:r0 + j + 1, :].astype(BF16).astype(F32)
            vj = svn_ref[r0 + j:r0 + j + 1, :].astype(BF16).astype(F32)
            z = jnp.sum(qf * kj, axis=-1, keepdims=True) + sbias
            w, nl = _sb_logs(jnp.where(tok_of_row > j, z, NEG_BIG))
            acc = acc + jnp.exp(w - nrun) * vj
            nrun = nrun + nl
        sacc_scr[s] = acc
        srun_scr[s] = nrun
    spage_sfx_scr[...] = _suffix_matrix(kbuf.shape[2])

    trips = pl.cdiv(n_visits + 1, SB_VISITS_PER_TRIP)
    per_trip = SB_PAGES_PER_TRIP
    n_total = seqs_per_step * n_pages
    ring = kbuf.shape[0]
    seq0 = (pl.program_id(0) * pl.num_programs(1) + pair) * seqs_per_step

    def page_copies(g):
        g = jnp.asarray(g, jnp.int32)
        gc = jnp.minimum(g, n_total - 1)
        pool = pt_ref[seq0 + lax.div(gc, jnp.int32(n_pages)), n_pages - 1 - lax.rem(gc, jnp.int32(n_pages))]
        slot = lax.rem(g, jnp.int32(ring))
        return (pltpu.make_async_copy(ck_hbm.at[pool], kbuf.at[slot], sems.at[0, slot]),
                pltpu.make_async_copy(cv_hbm.at[pool], vbuf.at[slot], sems.at[1, slot]))

    def start_pages(g0):
        for i in range(per_trip):
            for thread, copy in enumerate(page_copies(g0 + i)):
                copy.start(priority=thread)

    def wait_pages(g0):
        for i in range(per_trip):
            for copy in page_copies(g0 + i):
                copy.wait()

    def page_logits(g0):
        seq = lax.div(jnp.minimum(g0, n_total - 1), jnp.int32(n_pages))
        off = sbias + jnp.where(g0 < n_total, 0.0, NEG_BIG)
        return [_dot(sqbd_scr[seq], kbuf[lax.rem(g0 + i, jnp.int32(ring))].astype(BF16)) + off
                for i in range(SB_PAGE_GROUP)]

    def page_suffix(zs):
        logs = [_sb_logs(z) for z in zs]
        sums = [_sb_suffix(nl, spage_sfx_scr[...]) for _, nl in logs]
        return [(w, later, tot) for (w, _), (later, tot) in zip(logs, sums)]

    def page_finish(g0, parts):
        seq = lax.div(jnp.minimum(g0, n_total - 1), jnp.int32(n_pages))
        nrun = srun_scr[seq]
        acc = sacc_scr[seq]
        for i, (w, later, tot) in enumerate(parts):
            a = jnp.exp(w - later - nrun).astype(BF16)
            acc = acc + _dot_nt(a, vbuf[lax.rem(g0 + i, jnp.int32(ring))].astype(BF16))
            nrun = nrun + tot
        sacc_scr[seq] = acc
        srun_scr[seq] = nrun

    for ahead in range(SB_PREFETCH_TRIPS):
        start_pages(ahead * per_trip)

    groups = per_trip // SB_PAGE_GROUP
    visits_per_group = SB_VISITS_PER_TRIP // groups

    def body(u, carry):
        wait_pages(u * per_trip)
        start_pages((u + SB_PREFETCH_TRIPS) * per_trip)
        for k in range(groups):
            v0 = SB_VISITS_PER_TRIP * u + k * visits_per_group
            g0 = u * per_trip + k * SB_PAGE_GROUP
            zs = page_logits(g0)
            stage(v0, 0, 1)
            parts = page_suffix(zs)
            stage(v0 + 1, 1, 0)
            page_finish(g0, parts)
            for i in range(2, visits_per_group):
                stage(v0 + i, i % 2, (i + 1) % 2)
        return carry

    lax.fori_loop(0, trips, body, 0)
    for ahead in range(SB_PREFETCH_TRIPS):
        wait_pages((trips + ahead) * per_trip)

    for s in range(seqs_per_step):
        picked = jnp.where(own_head, sacc_scr[s], 0.0)
        for t in range(n_tok):
            so_ref[s * n_tok + t:s * n_tok + t + 1, :] = jnp.sum(
                picked[t * SB_HEADS:(t + 1) * SB_HEADS, :], axis=0, keepdims=True)


def _sb_attention(proj, proj_meta, proj_s, cache_k, cache_v, page_table, sb_bias, *, n_seq, seq_len, tile,
                  n_dec, n_tok):
    p3 = proj.reshape(n_seq, seq_len, PROJ_W)
    slab = lambda col: col // LANES
    n_main = seq_len // tile
    pair_rows = HEADS_PER_SLAB * tile
    n_pairs = SB_HEADS // HEADS_PER_SLAB
    width = SB_HEADS * SB_HEAD_DIM
    n_pages = page_table.shape[1]
    page_size = cache_k.shape[1]
    seqs_per_step = n_dec // (n_seq * n_pairs)
    assert seqs_per_step * n_seq * n_pairs == n_dec and n_pages % SB_PAGES_PER_TRIP == 0
    step_rows = seqs_per_step * n_tok
    n_rows = n_tok * SB_HEADS
    ck = cache_k.transpose(0, 2, 3, 1).reshape(cache_k.shape[0], width, page_size)
    cv = cache_v.transpose(0, 2, 3, 1).reshape(cache_v.shape[0], width, page_size)
    bias_rows = jnp.tile(sb_bias, n_tok).reshape(n_rows, 1)
    ring = (SB_PREFETCH_TRIPS + 1) * SB_PAGES_PER_TRIP
    visits = [(q, t) for q in range(n_main) for t in range(q + 2)]
    n_visits = len(visits)
    visits += [(n_main - 1, n_main + 1)] * (SB_VISITS_PER_TRIP + 2)
    vq = jnp.asarray([q for q, _ in visits], jnp.int32)
    vt = jnp.asarray([t for _, t in visits], jnp.int32)
    assert seqs_per_step * n_pages <= pl.cdiv(n_visits + 1, SB_VISITS_PER_TRIP) * SB_PAGES_PER_TRIP
    seq_blk = lambda col: pl.BlockSpec((None, seq_len, LANES), lambda b, p, *_: (b, 0, slab(col) + p))
    meta_blk = lambda col: pl.BlockSpec((LANES, LANES), lambda b, p, *_: (0, slab(col) + p))
    new_blk = lambda col: pl.BlockSpec((step_rows, width), lambda b, p, *_: (b * n_pairs + p, col // width))
    grid_spec = pltpu.PrefetchScalarGridSpec(
        num_scalar_prefetch=3,
        grid=(n_seq, n_pairs),
        in_specs=[pl.BlockSpec(memory_space=pltpu.SMEM),
                  seq_blk(COL_QB), seq_blk(COL_KB), seq_blk(COL_VB), meta_blk(COL_KB), meta_blk(COL_VB),
                  pl.BlockSpec((n_rows, 1), lambda b, p, *_: (0, 0)),
                  new_blk(COL_QB), new_blk(COL_KB), new_blk(COL_VB),
                  pl.BlockSpec(memory_space=pl.ANY), pl.BlockSpec(memory_space=pl.ANY)],
        out_specs=[pl.BlockSpec((None, seq_len, LANES), lambda b, p, *_: (b, 0, p)),
                   pl.BlockSpec((step_rows, width), lambda b, p, *_: (b * n_pairs + p, 0))],
        scratch_shapes=[pltpu.VMEM((n_main, pair_rows, LANES), BF16),
                        pltpu.VMEM((n_main + 1, tile, LANES), BF16),
                        pltpu.VMEM((n_main + 1, pair_rows, LANES), BF16),
                        pltpu.VMEM((4, pair_rows, tile), F32),
                        pltpu.VMEM((tile, tile), BF16),
                        pltpu.VMEM((2, pair_rows, tile), F32),
                        pltpu.VMEM((2, pair_rows, tile), F32),
                        pltpu.VMEM((2, pair_rows, tile), F32),
                        pltpu.VMEM((2, pair_rows, LANES), F32),
                        pltpu.VMEM((pair_rows, LANES), F32),
                        pltpu.VMEM((tile, LANES), F32),
                        pltpu.VMEM((seqs_per_step, n_rows, width), BF16),
                        pltpu.VMEM((seqs_per_step, n_rows, width), F32),
                        pltpu.VMEM((seqs_per_step, n_rows, 1), F32),
                        pltpu.VMEM((page_size, page_size), BF16),
                        pltpu.VMEM((ring, width, page_size), F32),
                        pltpu.VMEM((ring, width, page_size), F32),
                        pltpu.SemaphoreType.DMA((2, ring))],
    )
    out_p, out_s = pl.pallas_call(
        functools.partial(_sb_kernel, tile=tile, n_main=n_main, n_visits=n_visits, n_tok=n_tok,
                          seqs_per_step=seqs_per_step, n_pages=n_pages),
        grid_spec=grid_spec,
        out_shape=[jax.ShapeDtypeStruct((n_seq, seq_len, width), BF16),
                   jax.ShapeDtypeStruct((n_dec * n_tok, width), F32)],
        compiler_params=pltpu.CompilerParams(
            dimension_semantics=("arbitrary", "arbitrary"), vmem_limit_bytes=VMEM_LIMIT),
    )(vq, vt, page_table, sb_bias, p3, p3, p3, proj_meta, proj_meta, bias_rows, proj_s, proj_s, proj_s, ck, cv)
    return out_p.reshape(n_seq * seq_len, width), out_s


def _merge_kernel(oa_ref, ob_ref, ga_ref, gb_ref, h_ref, woa_ref, wob_ref, wout_ref, g_ref, out_ref):
    pa = _dot(oa_ref[...].astype(BF16), woa_ref[...])
    pb = _dot(ob_ref[...].astype(BF16), wob_ref[...])
    mixed = jax.nn.sigmoid(ga_ref[...]) * pa + jax.nn.sigmoid(gb_ref[...]) * pb
    out_ref[...] = h_ref[...] + _rms(_dot(mixed.astype(BF16), wout_ref[...]), g_ref[...])


def _merge(o_a, o_b, proj, h, w_o_gla, w_o_sb, w_out, g_post, *, tm):
    m, d = h.shape
    row = pl.BlockSpec((tm, d), lambda i: (i, 0))
    return pl.pallas_call(
        _merge_kernel,
        grid=(m // tm,),
        in_specs=[row, row,
                  pl.BlockSpec((tm, d), lambda i: (i, COL_GA // d)),
                  pl.BlockSpec((tm, d), lambda i: (i, COL_GB // d)),
                  row, _const_spec(w_o_gla.shape), _const_spec(w_o_sb.shape), _const_spec(w_out.shape),
                  _const_spec((1, d))],
        out_specs=row,
        out_shape=jax.ShapeDtypeStruct((m, d), F32),
        compiler_params=pltpu.CompilerParams(
            dimension_semantics=("arbitrary",), vmem_limit_bytes=VMEM_LIMIT),
    )(o_a, o_b, proj, proj, h, w_o_gla, w_o_sb, w_out, g_post)


def _row_tile(m, cap):
    return m if m <= cap else cap


def kernel(x_prompt, x_sample, cache_k, cache_v, state_gla, page_table, meta_tokens,
           ffn1_pre_g, ffn1_w_gate, ffn1_w_up, ffn1_w_down, ffn1_post_g,
           mix_pre_g, w_in, w_gk2, b_gk, gla_norm_g, sb_bias, w_o_gla, w_o_sb, w_out, mix_post_g,
           ffn2_pre_g, ffn2_w_gate, ffn2_w_up, ffn2_w_down, ffn2_post_g):
    n_seq, seq_len, d = x_prompt.shape
    n_dec, n_tok, _ = x_sample.shape
    assert ffn1_pre_g.shape[0] == 1, "single layer"
    qk = GLA_HEADS * GLA_DK
    v_w = GLA_HEADS * GLA_DV
    rank = w_gk2.shape[1]

    bf = lambda w: w[0].astype(BF16)
    w1g, w1u, w1d = bf(ffn1_w_gate), bf(ffn1_w_up), bf(ffn1_w_down)
    w2g, w2u, w2d = bf(ffn2_w_gate), bf(ffn2_w_up), bf(ffn2_w_down)
    w_in0 = w_in[0]
    gk0 = 2 * qk + v_w
    w_main = jnp.concatenate([w_in0[:, :gk0], w_in0[:, gk0 + rank:]], axis=1).astype(BF16)
    w_gk1 = jnp.pad(w_in0[:, gk0:gk0 + rank], ((0, 0), (0, LANES - rank))).astype(BF16)
    w_gk2p = jnp.pad(w_gk2[0], ((0, LANES - rank), (0, 0))).astype(BF16)
    woa, wob, wout = bf(w_o_gla), bf(w_o_sb), bf(w_out)
    row = lambda g: g[0].reshape(1, -1)
    bias = sb_bias[0]

    def pre_mix(x, tm):
        h, u = _ffn(x, row(ffn1_pre_g), w1g, w1u, w1d, row(ffn1_post_g), row(mix_pre_g),
                    tm=_row_tile(x.shape[0], FFN_ROW_GROUPS * tm))
        proj, glog = _proj(u, w_main, w_gk1, w_gk2p, row(b_gk), tm=_row_tile(x.shape[0], 2048), tn=1024)
        return h, u, proj, glog

    def post_mix(h, proj, o_a, o_b, tm):
        h2 = _merge(o_a, o_b, proj, h, woa, wob, wout, row(mix_post_g), tm=tm)
        return _ffn(h2, row(ffn2_pre_g), w2g, w2u, w2d, row(ffn2_post_g),
                    tm=_row_tile(h.shape[0], FFN_ROW_GROUPS * tm))

    x_meta = jnp.pad(meta_tokens.astype(F32), ((0, LANES - N_META), (0, 0)))
    _, u_m, proj_m, glog_m = pre_mix(x_meta, LANES)

    m_p = n_seq * seq_len
    tm_p = _row_tile(m_p, 512)
    h_p, u_p, proj_p, glog_p = pre_mix(x_prompt.reshape(m_p, d), tm_p)
    oa_p, s_p = _gla(proj_p, glog_p, row(gla_norm_g), n_seq=n_seq, seq_len=seq_len,
                     rows_per_step=_row_tile(seq_len, 512), chunk=GLA_CHUNK, n_heads=4,
                     meta=(proj_m, glog_m))

    m_s = n_dec * n_tok
    tm_s = _row_tile(m_s, 512)
    h_s, _, proj_s, glog_s = pre_mix(x_sample.reshape(m_s, d), tm_s)

    ob_p, ob_s = _sb_attention(proj_p, proj_m, proj_s, cache_k[0], cache_v[0], page_table, bias,
                               n_seq=n_seq, seq_len=seq_len, tile=MXU_DIM, n_dec=n_dec, n_tok=n_tok)
    y_p = post_mix(h_p, proj_p, oa_p, ob_p, tm_p)

    pad_tok = lambda a: jnp.pad(a.reshape(n_dec, n_tok, -1), ((0, 0), (0, SUBLANES - n_tok), (0, 0))
                                ).reshape(n_dec * SUBLANES, -1)
    oa_s, s_s = _gla(pad_tok(proj_s[:, :COL_QB]), pad_tok(glog_s), row(gla_norm_g), n_seq=n_dec,
                     seq_len=SUBLANES, rows_per_step=SUBLANES, chunk=SUBLANES, n_heads=GLA_HEADS,
                     n_seqs=4, state0=state_gla[0])
    oa_s = oa_s.reshape(n_dec, SUBLANES, v_w)[:, :n_tok].reshape(m_s, v_w)
    y_s = post_mix(h_s, proj_s, oa_s, ob_s, tm_s)

    heads = (SB_HEADS, SB_HEAD_DIM)
    w_sb = SB_HEADS * SB_HEAD_DIM

    u_full = jnp.concatenate([jnp.broadcast_to(u_m[:N_META][None], (n_seq, N_META, d)),
                              u_p.reshape(n_seq, seq_len, d)], axis=1)
    k_t, v_t = _kv_t(w_main[:, COL_KB:COL_KB + w_sb].T, w_main[:, COL_VB:COL_VB + w_sb].T, u_full,
                     tp=512)
    kv_out = lambda a: a.reshape(n_seq, *heads, N_META + seq_len).transpose(0, 3, 1, 2)[None]

    return (y_p.reshape(n_seq, seq_len, d),
            y_s.reshape(n_dec, n_tok, d),
            kv_out(k_t), kv_out(v_t),
            s_p[None],
            proj_s[:, COL_KB:COL_KB + w_sb].reshape(1, n_dec, n_tok, *heads),
            proj_s[:, COL_VB:COL_VB + w_sb].reshape(1, n_dec, n_tok, *heads),
            s_s[None])
```
